```python
import jax, jax.numpy as jnp
from jax import lax
import numpy as np

D_MODEL = 4096
BATCH = 4
SEQ = 2048
DEPTH = 2
DEC_BATCH = 8
DEC_SEQ = 1
PAST_LEN = 16384
PAGE_SIZE = 128

N_MIXERS = 2
N_A = (DEPTH + 1) // 2
N_B = DEPTH // 2
D_RNN = D_MODEL
LRU_BLOCKS = 16
LRU_BLOCK = D_RNN // LRU_BLOCKS
CONV_W = 4
LRU_C = 8.0
HEAD_DIM = 128
N_HEADS = D_MODEL // HEAD_DIM
GROUPS = ((128, 1), (512, 4), (2048, 16))
N_GROUPS = len(GROUPS)
D_FF = 4 * D_MODEL
ALPHA = (2.0 * DEPTH) ** 0.25
BETA = (8.0 * DEPTH) ** -0.25
LN_EPS = 1e-5

kernel_name = 'hybrid_rglru_dilated_swa_step'


def layer_norm(x, g, b):
    xf = x.astype(jnp.float32)
    mu = jnp.mean(xf, axis=-1, keepdims=True)
    var = jnp.mean(jnp.square(xf - mu), axis=-1, keepdims=True)
    y = (xf - mu) * lax.rsqrt(var + LN_EPS) * g.astype(jnp.float32) + b.astype(jnp.float32)
    return y.astype(x.dtype)


def sq_relu_mlp(x, w1, b1, w2, b2):
    h = jax.nn.relu(x @ w1 + b1)
    return (h * h) @ w2 + b2


def lru_scan(a, b, h0):
    def step(h, ab):
        at, bt = ab
        h = at * h + bt
        return h, h
    h_last, hs = lax.scan(step, h0, (jnp.swapaxes(a, 0, 1), jnp.swapaxes(b, 0, 1)))
    return jnp.swapaxes(hs, 0, 1), h_last


def rglru_mixer(x, conv_prev, h_prev, w_in, b_in, conv_w, conv_b, w_ra, b_ra, w_ix, b_ix, lam, w_out, b_out):
    Bn, T, _ = x.shape
    u = x @ w_in + b_in
    gate = jax.nn.gelu(u[..., :D_RNN])
    xb = u[..., D_RNN:]
    xp = jnp.concatenate([conv_prev.astype(xb.dtype), xb], axis=1)
    xc = conv_b + sum(conv_w[k] * xp[:, k:k + T] for k in range(CONV_W))
    new_conv = xp[:, T:]
    xr = xc.reshape(Bn, T, LRU_BLOCKS, LRU_BLOCK)
    r = jax.nn.sigmoid(jnp.einsum('btnc,ncd->btnd', xr, w_ra).reshape(Bn, T, D_RNN) + b_ra)
    i = jax.nn.sigmoid(jnp.einsum('btnc,ncd->btnd', xr, w_ix).reshape(Bn, T, D_RNN) + b_ix)
    log_a = -LRU_C * r.astype(jnp.float32) * jax.nn.softplus(-lam.astype(jnp.float32))
    a = jnp.exp(log_a)
    bt = jnp.sqrt(-jnp.expm1(2.0 * log_a)) * (i * xc).astype(jnp.float32)
    hs, h_last = lru_scan(a, bt, h_prev.astype(jnp.float32))
    y = (hs.astype(x.dtype) * gate) @ w_out + b_out
    return y, new_conv, h_last.astype(h_prev.dtype)


def dilated_attn_prompt(q, k, v, window, dilation):
    Bn, S, H, Dh = q.shape
    kb = window // dilation
    s_pad = -(-S // window) * window
    nb = s_pad // window

    def blocks(t):
        t = jnp.pad(t, ((0, 0), (0, s_pad - S), (0, 0), (0, 0)))
        return t.reshape(Bn, nb, kb, dilation, H, Dh)

    def with_prev(t):
        prev = jnp.pad(t[:, :-1], ((0, 0), (1, 0), (0, 0), (0, 0), (0, 0), (0, 0)))
        return jnp.concatenate([prev, t], axis=2)

    qb = blocks(q)
    kc = with_prev(blocks(k))
    vc = with_prev(blocks(v))
    s = jnp.einsum('bnqrhd,bnkrhd->bnrhqk', qb, kc) * (Dh ** -0.5)
    qi = jnp.arange(kb)[:, None]
    ki = jnp.arange(2 * kb)[None, :]
    dist = kb + qi - ki
    band = (dist >= 0) & (dist <= kb)
    inside = (jnp.arange(nb)[:, None, None] * kb + ki[None] - kb) >= 0
    mask = band[None] & inside
    s = jnp.where(mask[None, :, None, None], s, -jnp.inf)
    m = jnp.max(s, axis=-1, keepdims=True)
    p = jnp.exp(s - m)
    l = jnp.sum(p, axis=-1, keepdims=True)
    o = jnp.einsum('bnrhqk,bnkrhd->bnrhqd', p, vc) / l
    o = o.transpose(0, 1, 4, 2, 3, 5).reshape(Bn, s_pad, H, Dh)[:, :S]
    lse = (m + jnp.log(l))[..., 0].transpose(0, 1, 4, 2, 3).reshape(Bn, s_pad, H)[:, :S]
    return o, lse


def dilated_attn_sample(q, k_all, v_all, n_buf, window, dilation):
    T = q.shape[1]
    kb = window // dilation
    idx = n_buf + jnp.arange(T)[:, None] - jnp.arange(kb + 1)[None, :] * dilation
    valid = idx >= 0
    idx = jnp.maximum(idx, 0)
    kg = k_all[:, idx]
    vg = v_all[:, idx]
    s = jnp.einsum('bthd,btkhd->bthk', q, kg) * (q.shape[-1] ** -0.5)
    s = jnp.where(valid[None, :, None, :], s, -jnp.inf)
    lse = jax.nn.logsumexp(s, axis=-1)
    p = jnp.exp(s - lse[..., None])
    o = jnp.einsum('bthk,btkhd->bthd', p, vg)
    return o, lse


def merge_groups(outs, lses, w_o, dtype):
    wts = jax.nn.softmax(jnp.stack(lses), axis=0)
    o = jnp.einsum('gbth,gbthd->bthd', wts, jnp.stack(outs))
    Bn, T = o.shape[:2]
    return o.reshape(Bn, T, N_HEADS * HEAD_DIM).astype(dtype) @ w_o


def dilated_mixer_prompt(x, w_qkv, w_o):
    Bn, S, _ = x.shape
    qkv = (x @ w_qkv).reshape(Bn, S, N_GROUPS, 3, N_HEADS, HEAD_DIM)
    outs, lses, caches = [], [], []
    for g, (win, dil) in enumerate(GROUPS):
        q, k, v = qkv[:, :, g, 0], qkv[:, :, g, 1], qkv[:, :, g, 2]
        o, lse = dilated_attn_prompt(q.astype(jnp.float32), k.astype(jnp.float32), v.astype(jnp.float32), win, dil)
        outs.append(o)
        lses.append(lse)
        caches.append(jnp.stack([k, v], axis=2)[:, S - min(win, S):])
    return merge_groups(outs, lses, w_o, x.dtype), caches


def dilated_mixer_sample(x, bufs, w_qkv, w_o):
    Bn, T, _ = x.shape
    qkv = (x @ w_qkv).reshape(Bn, T, N_GROUPS, 3, N_HEADS, HEAD_DIM)
    outs, lses, caches = [], [], []
    for g, (win, dil) in enumerate(GROUPS):
        buf = bufs[g]
        n_buf = buf.shape[1]
        q = qkv[:, :, g, 0]
        kv_all = jnp.concatenate([buf, qkv[:, :, g, 1:3].astype(buf.dtype)], axis=1)
        o, lse = dilated_attn_sample(q.astype(jnp.float32), kv_all[:, :, 0].astype(jnp.float32),
                                     kv_all[:, :, 1].astype(jnp.float32), n_buf, win, dil)
        outs.append(o)
        lses.append(lse)
        caches.append(kv_all[:, T:])
    return merge_groups(outs, lses, w_o, x.dtype), caches


def trunk(x, is_sample, state_conv, state_h, kv_bufs,
          lru_w_in, lru_b_in, lru_conv_w, lru_conv_b, lru_w_ra, lru_b_ra, lru_w_ix, lru_b_ix,
          lru_lambda, lru_w_out, lru_b_out, attn_w_qkv, attn_w_o,
          mlp_w1, mlp_b1, mlp_w2, mlp_b2, ln_g, ln_b):
    Bn = x.shape[0]
    new_conv, new_h = [], []
    new_kv = [[] for _ in GROUPS]
    for i in range(DEPTH):
        j = i // N_MIXERS
        if i % N_MIXERS == 0:
            if is_sample:
                c_prev, h_prev = state_conv[j], state_h[j]
            else:
                c_prev = jnp.zeros((Bn, CONV_W - 1, D_RNN), x.dtype)
                h_prev = jnp.zeros((Bn, D_RNN), x.dtype)
            y, c, h = rglru_mixer(x, c_prev, h_prev, lru_w_in[j], lru_b_in[j], lru_conv_w[j], lru_conv_b[j],
                                  lru_w_ra[j], lru_b_ra[j], lru_w_ix[j], lru_b_ix[j], lru_lambda[j],
                                  lru_w_out[j], lru_b_out[j])
            new_conv.append(c)
            new_h.append(h)
        else:
            if is_sample:
                y, kvs = dilated_mixer_sample(x, [buf[j] for buf in kv_bufs], attn_w_qkv[j], attn_w_o[j])
            else:
                y, kvs = dilated_mixer_prompt(x, attn_w_qkv[j], attn_w_o[j])
            for g in range(N_GROUPS):
                new_kv[g].append(kvs[g])
        x = layer_norm(ALPHA * x + y, ln_g[i, 0], ln_b[i, 0])
        x = layer_norm(ALPHA * x + sq_relu_mlp(x, mlp_w1[i], mlp_b1[i], mlp_w2[i], mlp_b2[i]), ln_g[i, 1], ln_b[i, 1])
    return x, jnp.stack(new_conv), jnp.stack(new_h), [jnp.stack(n) for n in new_kv]


def setup_inputs(seed: int = 0) -> dict:
    key = jax.random.key(seed)
    ks = jax.random.split(key, 32)
    f32 = jnp.float32

    def nrm(k, shape, scale):
        return jax.random.normal(k, shape, f32) * scale

    l_buf = [min(w, PAST_LEN) for w, _ in GROUPS]
    u = jax.random.uniform(ks[10], (N_A, D_RNN), f32, 0.9, 0.999)
    s = u ** (1.0 / LRU_C)
    lru_lambda = jnp.log(s) - jnp.log1p(-s)
    return {
        'x_prompt': nrm(ks[0], (BATCH, SEQ, D_MODEL), 1.0),
        'x_sample': nrm(ks[1], (DEC_BATCH, DEC_SEQ, D_MODEL), 1.0),
        'state_conv': nrm(ks[2], (N_A, DEC_BATCH, CONV_W - 1, D_RNN), 1.0),
        'state_h': nrm(ks[3], (N_A, DEC_BATCH, D_RNN), 0.5),
        'cache_kv_w128': nrm(ks[4], (N_B, DEC_BATCH, l_buf[0], 2, N_HEADS, HEAD_DIM), 1.0),
        'cache_kv_w512': nrm(ks[5], (N_B, DEC_BATCH, l_buf[1], 2, N_HEADS, HEAD_DIM), 1.0),
        'cache_kv_w2048': nrm(ks[6], (N_B, DEC_BATCH, l_buf[2], 2, N_HEADS, HEAD_DIM), 1.0),
        'lru_w_in': nrm(ks[7], (N_A, D_MODEL, 2 * D_RNN), D_MODEL ** -0.5),
        'lru_b_in': nrm(ks[8], (N_A, 2 * D_RNN), 0.01),
        'lru_conv_w': nrm(ks[9], (N_A, CONV_W, D_RNN), CONV_W ** -0.5),
        'lru_conv_b': nrm(ks[11], (N_A, D_RNN), 0.01),
        'lru_w_ra': nrm(ks[12], (N_A, LRU_BLOCKS, LRU_BLOCK, LRU_BLOCK), LRU_BLOCK ** -0.5),
        'lru_b_ra': nrm(ks[13], (N_A, D_RNN), 0.01),
        'lru_w_ix': nrm(ks[14], (N_A, LRU_BLOCKS, LRU_BLOCK, LRU_BLOCK), LRU_BLOCK ** -0.5),
        'lru_b_ix': nrm(ks[15], (N_A, D_RNN), 0.01),
        'lru_lambda': lru_lambda,
        'lru_w_out': nrm(ks[16], (N_A, D_RNN, D_MODEL), BETA * D_RNN ** -0.5),
        'lru_b_out': nrm(ks[17], (N_A, D_MODEL), 0.01),
        'attn_w_qkv': nrm(ks[18], (N_B, D_MODEL, N_GROUPS * 3 * N_HEADS * HEAD_DIM), D_MODEL ** -0.5),
        'attn_w_o': nrm(ks[19], (N_B, N_HEADS * HEAD_DIM, D_MODEL), BETA * (N_HEADS * HEAD_DIM) ** -0.5),
        'mlp_w1': nrm(ks[20], (DEPTH, D_MODEL, D_FF), D_MODEL ** -0.5),
        'mlp_b1': nrm(ks[21], (DEPTH, D_FF), 0.01),
        'mlp_w2': nrm(ks[22], (DEPTH, D_FF, D_MODEL), BETA * D_FF ** -0.5),
        'mlp_b2': nrm(ks[23], (DEPTH, D_MODEL), 0.01),
        'ln_g': 1.0 + nrm(ks[24], (DEPTH, 2, D_MODEL), 0.01),
        'ln_b': nrm(ks[25], (DEPTH, 2, D_MODEL), 0.01),
    }


def reference(x_prompt, x_sample, state_conv, state_h, cache_kv_w128, cache_kv_w512, cache_kv_w2048,
              lru_w_in, lru_b_in, lru_conv_w, lru_conv_b, lru_w_ra, lru_b_ra, lru_w_ix, lru_b_ix,
              lru_lambda, lru_w_out, lru_b_out, attn_w_qkv, attn_w_o,
              mlp_w1, mlp_b1, mlp_w2, mlp_b2, ln_g, ln_b):
    weights = (lru_w_in, lru_b_in, lru_conv_w, lru_conv_b, lru_w_ra, lru_b_ra, lru_w_ix, lru_b_ix,
               lru_lambda, lru_w_out, lru_b_out, attn_w_qkv, attn_w_o,
               mlp_w1, mlp_b1, mlp_w2, mlp_b2, ln_g, ln_b)
    y_prompt, conv_p, h_p, kv_p = trunk(x_prompt, False, None, None, None, *weights)
    y_sample, conv_s, h_s, kv_s = trunk(x_sample, True, state_conv, state_h,
                                        [cache_kv_w128, cache_kv_w512, cache_kv_w2048], *weights)
    return (y_prompt, y_sample, conv_p, h_p, kv_p[0], kv_p[1], kv_p[2], conv_s, h_s, kv_s[0], kv_s[1], kv_s[2])
```

```python
import functools

import jax
import jax.numpy as jnp
from jax import lax
from jax.experimental import pallas as pl
from jax.experimental.pallas import tpu as pltpu

HEAD_DIM = 128
CONV_W = 4
LRU_C = 8.0
GROUPS = ((128, 1), (512, 4), (2048, 16))
KEYS_PER_BLOCK = 128
LN_EPS = 1e-5
DEPTH = 2
ALPHA = (2.0 * DEPTH) ** 0.25

V7X_SUBLANES = 8
V7X_BF16_SUBLANES = 16
V7X_VMEM_LIMIT_BYTES = 56 * 1024 * 1024

MM_TILE = 1024
MM_K_TILE = 2048
MM_FULL_K = 4096
LN_ROWS = 256
SCAN_CHUNK = 256

_BF16 = jnp.bfloat16
_F32 = jnp.float32


def _params(*sem):
    return pltpu.CompilerParams(dimension_semantics=sem, vmem_limit_bytes=V7X_VMEM_LIMIT_BYTES)


def _apply_act(acc, act):
    if act == "gelu":
        return jax.nn.gelu(acc)
    if act == "relu2":
        r = jnp.maximum(acc, 0.0)
        return r * r
    return acc


def _mm_kernel(x_ref, w_ref, b_ref, o_ref, *, act):
    acc = jnp.dot(x_ref[...], w_ref[...], preferred_element_type=_F32)
    acc = _apply_act(acc + b_ref[...], act)
    o_ref[...] = acc.astype(o_ref.dtype)


def _mm_acc_kernel(x_ref, w_ref, b_ref, o_ref, acc_ref, *, act, nk):
    k = pl.program_id(2)

    @pl.when(k == 0)
    def _():
        acc_ref[...] = jnp.zeros_like(acc_ref)

    acc_ref[...] += jnp.dot(x_ref[...], w_ref[...], preferred_element_type=_F32)

    @pl.when(k == nk - 1)
    def _():
        o_ref[...] = _apply_act(acc_ref[...] + b_ref[...], act).astype(o_ref.dtype)


def matmul(x, w, b, *, col_start=0, n_cols=None, act=None, out_dtype=_F32):
    m, kdim = x.shape
    n_cols = w.shape[1] - col_start if n_cols is None else n_cols
    tm = min(MM_TILE, m)
    tn = min(MM_TILE, n_cols)
    assert m % tm == 0 and n_cols % tn == 0 and col_start % tn == 0
    cb0 = col_start // tn
    if kdim <= MM_FULL_K:
        return pl.pallas_call(
            functools.partial(_mm_kernel, act=act),
            out_shape=jax.ShapeDtypeStruct((m, n_cols), out_dtype),
            grid=(n_cols // tn, m // tm),
            in_specs=[
                pl.BlockSpec((tm, kdim), lambda j, i: (i, 0)),
                pl.BlockSpec((kdim, tn), lambda j, i: (0, j + cb0)),
                pl.BlockSpec((1, tn), lambda j, i: (0, j + cb0)),
            ],
            out_specs=pl.BlockSpec((tm, tn), lambda j, i: (i, j)),
            compiler_params=_params("arbitrary", "arbitrary"),
            name="matmul",
        )(x, w, b)
    tk = MM_K_TILE
    assert kdim % tk == 0
    nk = kdim // tk
    return pl.pallas_call(
        functools.partial(_mm_acc_kernel, act=act, nk=nk),
        out_shape=jax.ShapeDtypeStruct((m, n_cols), out_dtype),
        grid=(n_cols // tn, m // tm, nk),
        in_specs=[
            pl.BlockSpec((tm, tk), lambda j, i, k: (i, k)),
            pl.BlockSpec((tk, tn), lambda j, i, k: (k, j + cb0)),
            pl.BlockSpec((1, tn), lambda j, i, k: (0, j + cb0)),
        ],
        out_specs=pl.BlockSpec((tm, tn), lambda j, i, k: (i, j)),
        scratch_shapes=[pltpu.VMEM((tm, tn), _F32)],
        compiler_params=_params("arbitrary", "arbitrary", "arbitrary"),
        name="matmul_kacc",
    )(x, w, b)


def _ln_kernel(x_ref, y_ref, g_ref, b_ref, o_ref, obf_ref):
    z = ALPHA * x_ref[...] + y_ref[...]
    mu = jnp.mean(z, axis=-1, keepdims=True)
    zc = z - mu
    var = jnp.mean(zc * zc, axis=-1, keepdims=True)
    out = zc * lax.rsqrt(var + LN_EPS) * g_ref[...] + b_ref[...]
    o_ref[...] = out
    obf_ref[...] = out.astype(_BF16)


def deepnorm(x, y, g, b):
    m, d = x.shape
    tm = min(LN_ROWS, m)
    assert m % tm == 0
    row = pl.BlockSpec((tm, d), lambda i: (i, 0))
    vec = pl.BlockSpec((1, d), lambda i: (0, 0))
    return pl.pallas_call(
        _ln_kernel,
        out_shape=(jax.ShapeDtypeStruct((m, d), _F32), jax.ShapeDtypeStruct((m, d), _BF16)),
        grid=(m // tm,),
        in_specs=[row, row, vec, vec],
        out_specs=(row, row),
        compiler_params=_params("arbitrary"),
        name="deepnorm",
    )(x, y, g, b)


def _softplus(x):
    return jnp.maximum(x, 0.0) + jnp.log1p(jnp.exp(-jnp.abs(x)))


def _lru_coeffs(xc, wra, bra, wix, bix, sp):
    xcb = xc.astype(_BF16)
    r = jax.nn.sigmoid(jnp.dot(xcb, wra, preferred_element_type=_F32) + bra)
    i = jax.nn.sigmoid(jnp.dot(xcb, wix, preferred_element_type=_F32) + bix)
    log_a = -LRU_C * r * sp
    a = jnp.exp(log_a)
    t = jnp.tanh(log_a)
    bt = jnp.sqrt(-2.0 * t / (1.0 - t)) * (i * xc)
    return a, bt


def _shift_rows(x, s, fill):
    rows = lax.broadcasted_iota(jnp.int32, x.shape, 0)
    return jnp.where(rows < s, fill, pltpu.roll(x, s, 0))


def _rglru_kernel(xb_ref, gate_ref, cw_ref, cb_ref, wra_ref, bra_ref, wix_ref, bix_ref, lam_ref,
                  o_ref, hlast_ref, xpad_ref, *, seq, chunk):
    pad = V7X_SUBLANES
    cw = cw_ref[...]
    cbias = cb_ref[...]
    wra = wra_ref[...].astype(_BF16)
    wix = wix_ref[...].astype(_BF16)
    bra = bra_ref[...]
    bix = bix_ref[...]
    sp = _softplus(-lam_ref[...])

    xpad_ref[pl.ds(0, pad), :] = jnp.zeros((pad, xpad_ref.shape[1]), _F32)
    xpad_ref[pl.ds(pad, seq), :] = xb_ref[...]

    def body(c, h):
        t0 = pl.multiple_of(c * chunk, chunk)
        xw = xpad_ref[pl.ds(t0, chunk + pad), :]
        xc = cbias
        for k in range(CONV_W):
            off = pad - (CONV_W - 1) + k
            xc = xc + cw[k:k + 1, :] * xw[off:off + chunk, :]
        a, bt = _lru_coeffs(xc, wra, bra, wix, bix, sp)
        s = 1
        while s < chunk:
            a_sh = _shift_rows(a, s, 1.0)
            b_sh = _shift_rows(bt, s, 0.0)
            bt = a * b_sh + bt
            a = a * a_sh
            s *= 2
        hs = a * h + bt
        o_ref[pl.ds(t0, chunk), :] = (hs * gate_ref[pl.ds(t0, chunk), :]).astype(o_ref.dtype)
        return hs[chunk - 1:chunk, :]

    h0 = jnp.zeros((1, xb_ref.shape[1]), _F32)
    hlast_ref[...] = lax.fori_loop(0, seq // chunk, body, h0)


def rglru_prompt(xb, gate, cw, cb, wra, bra, wix, bix, lam):
    bsz, seq, c = xb.shape
    nblk, blk, _ = wra.shape
    assert c == nblk * blk and seq % SCAN_CHUNK == 0
    col = lambda b, n: (b, 0, n)
    vec = pl.BlockSpec((1, blk), lambda b, n: (0, n))
    wspec = pl.BlockSpec((None, blk, blk), lambda b, n: (n, 0, 0))
    return pl.pallas_call(
        functools.partial(_rglru_kernel, seq=seq, chunk=SCAN_CHUNK),
        out_shape=(jax.ShapeDtypeStruct((bsz, seq, c), _BF16), jax.ShapeDtypeStruct((bsz, 1, c), _F32)),
        grid=(bsz, nblk),
        in_specs=[
            pl.BlockSpec((None, seq, blk), col),
            pl.BlockSpec((None, seq, blk), col),
            pl.BlockSpec((CONV_W, blk), lambda b, n: (0, n)),
            vec, wspec, vec, wspec, vec, vec,
        ],
        out_specs=(pl.BlockSpec((None, seq, blk), col), pl.BlockSpec((None, 1, blk), col)),
        scratch_shapes=[pltpu.VMEM((seq + V7X_SUBLANES, blk), _F32)],
        compiler_params=_params("arbitrary", "arbitrary"),
        name="rglru_prompt",
    )(xb, gate, cw, cb, wra, bra, wix, bix, lam)


def _rglru_step_kernel(xb_ref, gate_ref, c0_ref, c1_ref, c2_ref, h_ref, cw_ref, cb_ref,
                       wra_ref, bra_ref, wix_ref, bix_ref, lam_ref, o_ref, hnew_ref):
    cw = cw_ref[...]
    xc = (cb_ref[...] + cw[0:1, :] * c0_ref[...] + cw[1:2, :] * c1_ref[...]
          + cw[2:3, :] * c2_ref[...] + cw[3:4, :] * xb_ref[...])
    a, bt = _lru_coeffs(xc, wra_ref[...].astype(_BF16), bra_ref[...], wix_ref[...].astype(_BF16),
                        bix_ref[...], _softplus(-lam_ref[...]))
    h = a * h_ref[...] + bt
    hnew_ref[...] = h
    o_ref[...] = (h * gate_ref[...]).astype(o_ref.dtype)


def rglru_step(xb, gate, conv_prev, h_prev, cw, cb, wra, bra, wix, bix, lam):
    rows, c = xb.shape
    nblk, blk, _ = wra.shape
    col = pl.BlockSpec((rows, blk), lambda n: (0, n))
    vec = pl.BlockSpec((1, blk), lambda n: (0, n))
    wspec = pl.BlockSpec((None, blk, blk), lambda n: (n, 0, 0))
    return pl.pallas_call(
        _rglru_step_kernel,
        out_shape=(jax.ShapeDtypeStruct((rows, c), _BF16), jax.ShapeDtypeStruct((rows, c), _F32)),
        grid=(nblk,),
        in_specs=[col, col, col, col, col, col,
                  pl.BlockSpec((CONV_W, blk), lambda n: (0, n)), vec, wspec, vec, wspec, vec, vec],
        out_specs=(col, col),
        compiler_params=_params("arbitrary"),
        name="rglru_step",
    )(xb, gate, *conv_prev, h_prev, cw, cb, wra, bra, wix, bix, lam)


def _attn_prompt_kernel(*refs, seq):
    qkv_refs = refs[:9]
    o_ref = refs[9]
    acc_ref, m_ref, l_ref = refs[10:13]
    kb = KEYS_PER_BLOCK
    scale = HEAD_DIM ** -0.5
    qi = lax.broadcasted_iota(jnp.int32, (kb, kb), 0)
    ki = lax.broadcasted_iota(jnp.int32, (kb, kb), 1)
    cur_mask = ki <= qi
    prev_mask = ki >= qi

    for g, (_, dil) in enumerate(GROUPS):
        q_ref, k_ref, v_ref = qkv_refs[3 * g:3 * g + 3]
        per_res = seq // dil
        nblk = per_res // kb
        assert per_res % kb == 0

        def rows(start):
            return pl.ds(start, kb) if dil == 1 else pl.ds(start, kb, stride=dil)

        def block(idx, carry, q_ref=q_ref, k_ref=k_ref, v_ref=v_ref, nblk=nblk, dil=dil, rows=rows, g=g):
            if nblk == 1:
                r, j = idx, 0
            else:
                r, j = idx // nblk, idx % nblk
            start = r + j * (kb * dil)
            q = q_ref[rows(start), :].astype(_BF16)
            k = k_ref[rows(start), :].astype(_BF16)
            v = v_ref[rows(start), :].astype(_BF16)
            s = lax.dot_general(q, k, (((1,), (1,)), ((), ())), preferred_element_type=_F32) * scale
            s = jnp.where(cur_mask, s, -jnp.inf)
            m = jnp.max(s, axis=-1, keepdims=True)
            if nblk > 1:
                pstart = jnp.maximum(start - kb * dil, r)
                kp = k_ref[rows(pstart), :].astype(_BF16)
                vp = v_ref[rows(pstart), :].astype(_BF16)
                sp = lax.dot_general(q, kp, (((1,), (1,)), ((), ())), preferred_element_type=_F32) * scale
                sp = jnp.where(jnp.logical_and(prev_mask, j > 0), sp, -jnp.inf)
                m = jnp.maximum(m, jnp.max(sp, axis=-1, keepdims=True))
            p = jnp.exp(s - m)
            l = jnp.sum(p, axis=-1, keepdims=True)
            acc = jnp.dot(p.astype(_BF16), v, preferred_element_type=_F32)
            if nblk > 1:
                pp = jnp.exp(sp - m)
                l = l + jnp.sum(pp, axis=-1, keepdims=True)
                acc = acc + jnp.dot(pp.astype(_BF16), vp, preferred_element_type=_F32)
            acc_ref[g, rows(start), :] = acc
            m_ref[g, rows(start), :] = jnp.broadcast_to(m, (kb, HEAD_DIM))
            l_ref[g, rows(start), :] = jnp.broadcast_to(l, (kb, HEAD_DIM))
            return carry

        lax.fori_loop(0, seq // kb, block, 0)

    def merge(c, carry):
        t0 = pl.multiple_of(c * kb, kb)
        ms = [m_ref[g, pl.ds(t0, kb), :] for g in range(len(GROUPS))]
        mx = functools.reduce(jnp.maximum, ms)
        num = jnp.zeros((kb, HEAD_DIM), _F32)
        den = jnp.zeros((kb, HEAD_DIM), _F32)
        for g in range(len(GROUPS)):
            w = jnp.exp(ms[g] - mx)
            num = num + w * acc_ref[g, pl.ds(t0, kb), :]
            den = den + w * l_ref[g, pl.ds(t0, kb), :]
        o_ref[pl.ds(t0, kb), :] = (num / den).astype(o_ref.dtype)
        return carry

    lax.fori_loop(0, seq // kb, merge, 0)


def attn_prompt(qkv, n_heads):
    bsz, seq, _ = qkv.shape
    ng = len(GROUPS)

    def spec(section):
        return pl.BlockSpec((None, seq, HEAD_DIM), lambda b, h: (b, 0, section * n_heads + h))

    return pl.pallas_call(
        functools.partial(_attn_prompt_kernel, seq=seq),
        out_shape=jax.ShapeDtypeStruct((bsz, seq, n_heads * HEAD_DIM), _BF16),
        grid=(bsz, n_heads),
        in_specs=[spec(s) for s in range(3 * ng)],
        out_specs=pl.BlockSpec((None, seq, HEAD_DIM), lambda b, h: (b, 0, h)),
        scratch_shapes=[pltpu.VMEM((ng, seq, HEAD_DIM), _F32)] * 3,
        compiler_params=_params("arbitrary", "arbitrary"),
        name="attn_prompt",
    )(*([qkv] * (3 * ng)))


def _attn_sample_kernel(*refs):
    ng = len(GROUPS)
    q_refs = refs[0:ng]
    knew_refs = refs[ng:2 * ng]
    vnew_refs = refs[2 * ng:3 * ng]
    kc_refs = refs[3 * ng:4 * ng]
    vc_refs = refs[4 * ng:5 * ng]
    o_ref = refs[5 * ng]
    scale = HEAD_DIM ** -0.5
    parts = []
    for g in range(ng):
        q = q_refs[g][...]
        s_new = jnp.sum(q * knew_refs[g][...], axis=-1, keepdims=True) * scale
        s_old = jnp.sum(q[None] * kc_refs[g][...], axis=-1, keepdims=True) * scale
        m = jnp.maximum(jnp.max(s_old, axis=0), s_new)
        p_new = jnp.exp(s_new - m)
        p_old = jnp.exp(s_old - m[None])
        l = p_new + jnp.sum(p_old, axis=0)
        acc = p_new * vnew_refs[g][...] + jnp.sum(p_old * vc_refs[g][...], axis=0)
        parts.append((m, l, acc))
    mx = functools.reduce(jnp.maximum, [p[0] for p in parts])
    num = sum(jnp.exp(m - mx) * acc for m, _, acc in parts)
    den = sum(jnp.exp(m - mx) * l for m, l, _ in parts)
    o_ref[...] = (num / den).astype(o_ref.dtype)


def attn_sample(qkv, caches, n_heads):
    bsz = qkv.shape[0]
    ng = len(GROUPS)
    kb = KEYS_PER_BLOCK

    def new_spec(section):
        return pl.BlockSpec((None, None, n_heads, HEAD_DIM), lambda b: (b, section, 0, 0))

    def cache_spec(kv):
        return pl.BlockSpec((None, kb, None, None, n_heads, HEAD_DIM), lambda b: (b, 0, 0, kv, 0, 0))

    strided = []
    for (win, dil), cache in zip(GROUPS, caches):
        assert cache.shape[1] == win == kb * dil
        strided.append(cache.reshape(bsz, kb, dil, 2, n_heads, HEAD_DIM))
    in_specs = ([new_spec(3 * g) for g in range(ng)] + [new_spec(3 * g + 1) for g in range(ng)]
                + [new_spec(3 * g + 2) for g in range(ng)] + [cache_spec(0)] * ng + [cache_spec(1)] * ng)
    return pl.pallas_call(
        _attn_sample_kernel,
        out_shape=jax.ShapeDtypeStruct((bsz, n_heads, HEAD_DIM), _BF16),
        grid=(bsz,),
        in_specs=in_specs,
        out_specs=pl.BlockSpec((None, n_heads, HEAD_DIM), lambda b: (b, 0, 0)),
        compiler_params=_params("arbitrary"),
        name="attn_sample",
    )(*([qkv] * (3 * ng)), *strided, *strided)


def _cache_roll_kernel(*refs, n_caches):
    old = refs[:n_caches]
    new = refs[n_caches:2 * n_caches]
    out = refs[2 * n_caches:3 * n_caches]
    sems = refs[3 * n_caches]
    copies = []
    for c in range(n_caches):
        length = old[c].shape[1]
        copies.append(pltpu.make_async_copy(old[c].at[:, pl.ds(1, length - 1)],
                                            out[c].at[:, pl.ds(0, length - 1)], sems.at[c, 0]))
        copies.append(pltpu.make_async_copy(new[c], out[c].at[:, pl.ds(length - 1, 1)], sems.at[c, 1]))
    for cp in copies:
        cp.start()
    for cp in copies:
        cp.wait()


def cache_roll(caches, new_rows):
    n = len(caches)
    any_spec = pl.BlockSpec(memory_space=pl.ANY)
    return pl.pallas_call(
        functools.partial(_cache_roll_kernel, n_caches=n),
        out_shape=tuple(jax.ShapeDtypeStruct(c.shape, c.dtype) for c in caches),
        in_specs=[any_spec] * (2 * n),
        out_specs=tuple([any_spec] * n),
        scratch_shapes=[pltpu.SemaphoreType.DMA((n, 2))],
        name="cache_roll",
    )(*caches, *new_rows)


def _trunk(x, weights, sample_state):
    (w_in, b_in, conv_w, conv_b, w_ra, b_ra, w_ix, b_ix, lam, w_out, b_out, w_qkv, w_o,
     w1, b1, w2, b2, ln_g, ln_b) = weights
    bsz, seq, d = x.shape
    n_heads = d // HEAD_DIM
    is_sample = sample_state is not None
    rows = bsz * seq
    x2d = x.reshape(rows, d)
    if is_sample:
        assert seq == 1
        pad_rows = -rows % V7X_BF16_SUBLANES
        x2d = jnp.pad(x2d, ((0, pad_rows), (0, 0)))
    m = x2d.shape[0]
    xbf = x2d.astype(_BF16)

    def mlp(xf, xb, i):
        h = matmul(xb, w1[i], b1[i], act="relu2", out_dtype=_BF16)
        y = matmul(h, w2[i], b2[i])
        return deepnorm(xf, y, ln_g[i][1], ln_b[i][1])

    d_rnn = conv_w[0].shape[1]
    gate = matmul(xbf, w_in[0], b_in[0], col_start=0, n_cols=d_rnn, act="gelu")
    xb = matmul(xbf, w_in[0], b_in[0], col_start=d_rnn, n_cols=d_rnn)
    lru_args = (conv_w[0], conv_b[0], w_ra[0], b_ra[0], w_ix[0], b_ix[0], lam[0])
    if is_sample:
        state_conv, state_h, caches = sample_state
        padr = lambda a: jnp.pad(a, ((0, m - rows), (0, 0)))
        conv_prev = [padr(state_conv[0][:, k]) for k in range(CONV_W - 1)]
        hg, h_new = rglru_step(xb, gate, conv_prev, padr(state_h[0]), *lru_args)
        new_conv = jnp.concatenate([state_conv[0][:, 1:], xb[:rows, None, :]], axis=1)
        h_last = h_new[:rows]
    else:
        hg, h_last = rglru_prompt(xb.reshape(bsz, seq, d_rnn), gate.reshape(bsz, seq, d_rnn), *lru_args)
        hg = hg.reshape(m, d_rnn)
        h_last = h_last.reshape(bsz, d_rnn)
        new_conv = xb.reshape(bsz, seq, d_rnn)[:, seq - (CONV_W - 1):]
    y = matmul(hg, w_out[0], b_out[0])
    xf, xb16 = deepnorm(x2d, y, ln_g[0][0], ln_b[0][0])
    xf, xb16 = mlp(xf, xb16, 0)

    ng = len(GROUPS)
    zero_bias = jnp.zeros((1, w_qkv[0].shape[1]), _F32)
    qkv = matmul(xb16, w_qkv[0], zero_bias)
    kv_width = 2 * n_heads * HEAD_DIM
    sec = n_heads * HEAD_DIM
    if is_sample:
        qkv_s = qkv[:rows].reshape(bsz, 3 * ng, n_heads, HEAD_DIM)
        o = attn_sample(qkv_s, [c[0] for c in caches], n_heads).reshape(rows, d)
        o = jnp.pad(o, ((0, m - rows), (0, 0)))
        new_rows = [qkv_s[:, 3 * g + 1:3 * g + 3].reshape(bsz, 1, 2, n_heads, HEAD_DIM) for g in range(ng)]
        new_kv = [c[None] for c in cache_roll([c[0] for c in caches], new_rows)]
    else:
        o = attn_prompt(qkv.reshape(bsz, seq, -1), n_heads).reshape(m, d)
        new_kv = []
        for g, (win, _) in enumerate(GROUPS):
            keep = min(win, seq)
            kv = qkv.reshape(bsz, seq, -1)[:, seq - keep:, 3 * g * sec + sec:3 * g * sec + sec + kv_width]
            new_kv.append(kv.reshape(1, bsz, keep, 2, n_heads, HEAD_DIM))
    y = matmul(o, w_o[0], jnp.zeros((1, d), _F32))
    xf, xb16 = deepnorm(xf, y, ln_g[1][0], ln_b[1][0])
    xf, xb16 = mlp(xf, xb16, 1)

    out = xf[:rows].reshape(bsz, seq, d)
    return out, new_conv[None], h_last[None], new_kv


def kernel(x_prompt, x_sample, state_conv, state_h, cache_kv_w128, cache_kv_w512, cache_kv_w2048,
           lru_w_in, lru_b_in, lru_conv_w, lru_conv_b, lru_w_ra, lru_b_ra, lru_w_ix, lru_b_ix,
           lru_lambda, lru_w_out, lru_b_out, attn_w_qkv, attn_w_o,
           mlp_w1, mlp_b1, mlp_w2, mlp_b2, ln_g, ln_b):
    bf = lambda w: w.astype(_BF16)
    row = lambda v: v[:, None, :]
    weights = (bf(lru_w_in), row(lru_b_in), lru_conv_w, row(lru_conv_b), lru_w_ra, row(lru_b_ra),
               lru_w_ix, row(lru_b_ix), row(lru_lambda), bf(lru_w_out), row(lru_b_out),
               bf(attn_w_qkv), bf(attn_w_o), bf(mlp_w1), row(mlp_b1), bf(mlp_w2), row(mlp_b2),
               ln_g[:, :, None, :], ln_b[:, :, None, :])
    y_p, conv_p, h_p, kv_p = _trunk(x_prompt, weights, None)
    y_s, conv_s, h_s, kv_s = _trunk(x_sample, weights,
                                    (state_conv, state_h, (cache_kv_w128, cache_kv_w512, cache_kv_w2048)))
    return (y_p, y_s, conv_p, h_p, kv_p[0], kv_p[1], kv_p[2], conv_s, h_s, kv_s[0], kv_s[1], kv_s[2])
```

```python
import functools

import jax
import jax.numpy as jnp
from jax import lax
from jax.experimental import pallas as pl
from jax.experimental.pallas import tpu as pltpu

HEAD_DIM = 128
CONV_W = 4
LRU_C = 8.0
GROUPS = ((128, 1), (512, 4), (2048, 16))
KEYS_PER_BLOCK = 128
LN_EPS = 1e-5
DEPTH = 2
ALPHA = (2.0 * DEPTH) ** 0.25

V7X_SUBLANES = 8
V7X_BF16_SUBLANES = 16
V7X_VMEM_LIMIT_BYTES = 56 * 1024 * 1024

MM_TILE = 1024
MM_SMALL_M_TILE_N = 512
MM_FULL_K = 4096
MM_K_TILE = 1024
MM_KACC_TILE_M = 2048
ATTN_UNROLL = 8
ATTN_MAX_STRIDE = 4
CACHE_ROLL_ROWS = 128
LN_ROWS = 256
SCAN_CHUNK = 256

_BF16 = jnp.bfloat16
_F32 = jnp.float32


def _params(*sem):
    return pltpu.CompilerParams(dimension_semantics=sem, vmem_limit_bytes=V7X_VMEM_LIMIT_BYTES)


def _apply_act(acc, act):
    if act == "gelu":
        return jax.nn.gelu(acc)
    if act == "relu2":
        r = jnp.maximum(acc, 0.0)
        return r * r
    return acc


def _mm_kernel(x_ref, wc_ref, b_ref, o_ref, wbf_ref, *, act, act_tiles, ck, nj):
    j = pl.program_id(0)
    i = pl.program_id(1)

    @pl.when(j < nj)
    def _():
        r0 = pl.multiple_of(i * ck, ck)
        wbf_ref[j % 2, pl.ds(r0, ck), :] = wc_ref[...].astype(_BF16)

    @pl.when(j > 0)
    def _():
        acc = jnp.dot(x_ref[...], wbf_ref[(j - 1) % 2], preferred_element_type=_F32) + b_ref[...]
        if act_tiles is None:
            o_ref[...] = _apply_act(acc, act).astype(o_ref.dtype)
        else:
            @pl.when(j - 1 < act_tiles)
            def _():
                o_ref[...] = _apply_act(acc, act).astype(o_ref.dtype)

            @pl.when(j - 1 >= act_tiles)
            def _():
                o_ref[...] = acc.astype(o_ref.dtype)


def _mm_acc_kernel(x_ref, w_ref, b_ref, o_ref, acc_ref, *, act, nk):
    k = pl.program_id(2)

    @pl.when(k == 0)
    def _():
        acc_ref[...] = jnp.zeros_like(acc_ref)

    acc_ref[...] += jnp.dot(x_ref[...], w_ref[...].astype(_BF16), preferred_element_type=_F32)

    @pl.when(k == nk - 1)
    def _():
        o_ref[...] = _apply_act(acc_ref[...] + b_ref[...], act).astype(o_ref.dtype)


def matmul(x, w, layer, b, *, act=None, act_cols=None, out_dtype=_F32):
    m, kdim = x.shape
    n = w.shape[2]
    if kdim <= MM_FULL_K:
        tm = min(MM_TILE, m)
        tn = min(MM_TILE if m >= MM_TILE else MM_SMALL_M_TILE_N, n)
        assert m % tm == 0 and n % tn == 0
        ni, nj = m // tm, n // tn
        assert kdim % ni == 0
        ck = kdim // ni
        act_tiles = None
        if act_cols is not None:
            assert act_cols % tn == 0
            act_tiles = act_cols // tn
        return pl.pallas_call(
            functools.partial(_mm_kernel, act=act, act_tiles=act_tiles, ck=ck, nj=nj),
            out_shape=jax.ShapeDtypeStruct((m, n), out_dtype),
            grid=(nj + 1, ni),
            in_specs=[
                pl.BlockSpec((tm, kdim), lambda j, i: (jnp.where(j == 0, 0, i), 0)),
                pl.BlockSpec((None, ck, tn),
                             lambda j, i: (layer, jnp.where(j < nj, i, ni - 1), jnp.minimum(j, nj - 1))),
                pl.BlockSpec((1, tn), lambda j, i: (0, jnp.maximum(j - 1, 0))),
            ],
            out_specs=pl.BlockSpec((tm, tn), lambda j, i: (jnp.where(j == 0, 0, i), jnp.maximum(j - 1, 0))),
            scratch_shapes=[pltpu.VMEM((2, kdim, tn), _BF16)],
            compiler_params=_params("arbitrary", "arbitrary"),
            name="matmul",
        )(x, w, b)
    assert act_cols is None
    tm = min(MM_KACC_TILE_M, m)
    tn = min(MM_TILE, n)
    tk = MM_K_TILE
    assert m % tm == 0 and n % tn == 0 and kdim % tk == 0
    nk = kdim // tk
    return pl.pallas_call(
        functools.partial(_mm_acc_kernel, act=act, nk=nk),
        out_shape=jax.ShapeDtypeStruct((m, n), out_dtype),
        grid=(n // tn, m // tm, nk),
        in_specs=[
            pl.BlockSpec((tm, tk), lambda j, i, k: (i, k)),
            pl.BlockSpec((None, tk, tn), lambda j, i, k: (layer, k, j)),
            pl.BlockSpec((1, tn), lambda j, i, k: (0, j)),
        ],
        out_specs=pl.BlockSpec((tm, tn), lambda j, i, k: (i, j)),
        scratch_shapes=[pltpu.VMEM((tm, tn), _F32)],
        compiler_params=_params("arbitrary", "arbitrary", "arbitrary"),
        name="matmul_kacc",
    )(x, w, b)


def _ln_kernel(x_ref, y_ref, g_ref, b_ref, o_ref, obf_ref):
    z = ALPHA * x_ref[...] + y_ref[...]
    mu = jnp.mean(z, axis=-1, keepdims=True)
    zc = z - mu
    var = jnp.mean(zc * zc, axis=-1, keepdims=True)
    out = zc * lax.rsqrt(var + LN_EPS) * g_ref[...] + b_ref[...]
    o_ref[...] = out
    obf_ref[...] = out.astype(_BF16)


def deepnorm(x, y, g, b):
    m, d = x.shape
    tm = min(LN_ROWS, m)
    assert m % tm == 0
    row = pl.BlockSpec((tm, d), lambda i: (i, 0))
    vec = pl.BlockSpec((1, d), lambda i: (0, 0))
    return pl.pallas_call(
        _ln_kernel,
        out_shape=(jax.ShapeDtypeStruct((m, d), _F32), jax.ShapeDtypeStruct((m, d), _BF16)),
        grid=(m // tm,),
        in_specs=[row, row, vec, vec],
        out_specs=(row, row),
        compiler_params=_params("arbitrary"),
        name="deepnorm",
    )(x, y, g, b)


def _softplus(x):
    return jnp.maximum(x, 0.0) + jnp.log1p(jnp.exp(-jnp.abs(x)))


def _lru_coeffs(xc, wra, bra, wix, bix, sp):
    xcb = xc.astype(_BF16)
    r = jax.nn.sigmoid(jnp.dot(xcb, wra, preferred_element_type=_F32) + bra)
    i = jax.nn.sigmoid(jnp.dot(xcb, wix, preferred_element_type=_F32) + bix)
    log_a = -LRU_C * r * sp
    a = jnp.exp(log_a)
    t = jnp.tanh(log_a)
    bt = jnp.sqrt(-2.0 * t / (1.0 - t)) * (i * xc)
    return a, bt


def _shift_rows(x, s, fill):
    rows = lax.broadcasted_iota(jnp.int32, x.shape, 0)
    return jnp.where(rows < s, fill, pltpu.roll(x, s, 0))


def _rglru_kernel(xb_ref, gate_ref, cw_ref, cb_ref, wra_ref, bra_ref, wix_ref, bix_ref, lam_ref,
                  o_ref, hlast_ref, xpad_ref, *, seq, chunk):
    pad = V7X_SUBLANES
    cw = cw_ref[...]
    cbias = cb_ref[...]
    wra = wra_ref[...].astype(_BF16)
    wix = wix_ref[...].astype(_BF16)
    bra = bra_ref[...]
    bix = bix_ref[...]
    sp = _softplus(-lam_ref[...])

    xpad_ref[pl.ds(0, pad), :] = jnp.zeros((pad, xpad_ref.shape[1]), _F32)
    xpad_ref[pl.ds(pad, seq), :] = xb_ref[...]

    def body(c, h):
        t0 = pl.multiple_of(c * chunk, chunk)
        xw = xpad_ref[pl.ds(t0, chunk + pad), :]
        xc = cbias
        for k in range(CONV_W):
            off = pad - (CONV_W - 1) + k
            xc = xc + cw[k:k + 1, :] * xw[off:off + chunk, :]
        a, bt = _lru_coeffs(xc, wra, bra, wix, bix, sp)
        s = 1
        while s < chunk:
            a_sh = _shift_rows(a, s, 1.0)
            b_sh = _shift_rows(bt, s, 0.0)
            bt = a * b_sh + bt
            a = a * a_sh
            s *= 2
        hs = a * h + bt
        o_ref[pl.ds(t0, chunk), :] = (hs * gate_ref[pl.ds(t0, chunk), :]).astype(o_ref.dtype)
        return hs[chunk - 1:chunk, :]

    h0 = jnp.zeros((1, xb_ref.shape[1]), _F32)
    hlast_ref[...] = lax.fori_loop(0, seq // chunk, body, h0)


def rglru_prompt(u, cw, cb, wra, bra, wix, bix, lam):
    bsz, seq, c2 = u.shape
    c = c2 // 2
    nblk, blk, _ = wra.shape
    assert c == nblk * blk and seq % SCAN_CHUNK == 0
    col = lambda b, n: (b, 0, n)
    vec = pl.BlockSpec((1, blk), lambda b, n: (0, n))
    wspec = pl.BlockSpec((None, blk, blk), lambda b, n: (n, 0, 0))
    return pl.pallas_call(
        functools.partial(_rglru_kernel, seq=seq, chunk=SCAN_CHUNK),
        out_shape=(jax.ShapeDtypeStruct((bsz, seq, c), _BF16), jax.ShapeDtypeStruct((bsz, 1, c), _F32)),
        grid=(bsz, nblk),
        in_specs=[
            pl.BlockSpec((None, seq, blk), lambda b, n: (b, 0, n + nblk)),
            pl.BlockSpec((None, seq, blk), col),
            pl.BlockSpec((CONV_W, blk), lambda b, n: (0, n)),
            vec, wspec, vec, wspec, vec, vec,
        ],
        out_specs=(pl.BlockSpec((None, seq, blk), col), pl.BlockSpec((None, 1, blk), col)),
        scratch_shapes=[pltpu.VMEM((seq + V7X_SUBLANES, blk), _F32)],
        compiler_params=_params("arbitrary", "arbitrary"),
        name="rglru_prompt",
    )(u, u, cw, cb, wra, bra, wix, bix, lam)


def _rglru_step_kernel(xb_ref, gate_ref, c0_ref, c1_ref, c2_ref, h_ref, cw_ref, cb_ref,
                       wra_ref, bra_ref, wix_ref, bix_ref, lam_ref, o_ref, hnew_ref):
    cw = cw_ref[...]
    xc = (cb_ref[...] + cw[0:1, :] * c0_ref[...] + cw[1:2, :] * c1_ref[...]
          + cw[2:3, :] * c2_ref[...] + cw[3:4, :] * xb_ref[...])
    a, bt = _lru_coeffs(xc, wra_ref[...].astype(_BF16), bra_ref[...], wix_ref[...].astype(_BF16),
                        bix_ref[...], _softplus(-lam_ref[...]))
    h = a * h_ref[...] + bt
    hnew_ref[...] = h
    o_ref[...] = (h * gate_ref[...]).astype(o_ref.dtype)


def rglru_step(u, conv_prev, h_prev, cw, cb, wra, bra, wix, bix, lam):
    rows, c2 = u.shape
    c = c2 // 2
    nblk, blk, _ = wra.shape
    col = pl.BlockSpec((rows, blk), lambda n: (0, n))
    vec = pl.BlockSpec((1, blk), lambda n: (0, n))
    wspec = pl.BlockSpec((None, blk, blk), lambda n: (n, 0, 0))
    return pl.pallas_call(
        _rglru_step_kernel,
        out_shape=(jax.ShapeDtypeStruct((rows, c), _BF16), jax.ShapeDtypeStruct((rows, c), _F32)),
        grid=(nblk,),
        in_specs=[pl.BlockSpec((rows, blk), lambda n: (0, n + nblk)), col, col, col, col, col,
                  pl.BlockSpec((CONV_W, blk), lambda n: (0, n)), vec, wspec, vec, wspec, vec, vec],
        out_specs=(col, col),
        compiler_params=_params("arbitrary"),
        name="rglru_step",
    )(u, u, *conv_prev, h_prev, cw, cb, wra, bra, wix, bix, lam)


def _attn_prompt_kernel(*refs, seq):
    qkv_refs = refs[:9]
    o_ref = refs[9]
    acc_ref, m_ref, l_ref, stage_in_ref, stage_out_ref = refs[10:15]
    kb = KEYS_PER_BLOCK
    scale = HEAD_DIM ** -0.5
    qi = lax.broadcasted_iota(jnp.int32, (kb, kb), 0)
    ki = lax.broadcasted_iota(jnp.int32, (kb, kb), 1)
    cur_mask = ki <= qi
    qi2 = lax.broadcasted_iota(jnp.int32, (kb, 2 * kb), 0)
    ki2 = lax.broadcasted_iota(jnp.int32, (kb, 2 * kb), 1)
    cur_part = jnp.logical_and(ki2 >= kb, ki2 - kb <= qi2)
    prev_part = jnp.logical_and(ki2 < kb, ki2 >= qi2)
    n_blocks = seq // kb
    assert n_blocks % ATTN_UNROLL == 0

    def strided(start, n, st):
        return pl.ds(start, n) if st == 1 else pl.ds(start, n, stride=st)

    def run_group(load, store, st, seg_len):
        per_res = seg_len // st
        nblk = per_res // kb
        assert per_res % kb == 0 and n_blocks % (st * nblk) == 0

        def block_rows(idx):
            seg = idx // (st * nblk)
            r = (idx // nblk) % st
            j = idx % nblk
            first = seg * seg_len + r
            start = first + j * (kb * st)
            prows = strided(jnp.maximum(start - kb * st, first), kb, st)
            return strided(start, kb, st), prows, j

        def scores(rows, prows, j):
            q = load(0, rows).astype(_BF16)
            k = load(1, rows).astype(_BF16)
            if nblk > 1:
                k = jnp.concatenate([load(1, prows).astype(_BF16), k], axis=0)
                mask = jnp.logical_or(cur_part, jnp.logical_and(prev_part, j > 0))
            else:
                mask = cur_mask
            s = lax.dot_general(q, k, (((1,), (1,)), ((), ())), preferred_element_type=_F32) * scale
            return jnp.where(mask, s, -jnp.inf)

        def values(rows, prows):
            v = load(2, rows).astype(_BF16)
            if nblk > 1:
                v = jnp.concatenate([load(2, prows).astype(_BF16), v], axis=0)
            return v

        def blocks(it, carry):
            where = [block_rows(it * ATTN_UNROLL + u) for u in range(ATTN_UNROLL)]
            ss = [scores(rows, prows, j) for rows, prows, j in where]
            ms = [jnp.max(s, axis=-1, keepdims=True) for s in ss]
            ps = [jnp.exp(s - m) for s, m in zip(ss, ms)]
            for (rows, prows, _), m, p in zip(where, ms, ps):
                store(0, rows, jnp.dot(p.astype(_BF16), values(rows, prows), preferred_element_type=_F32))
                store(1, rows, jnp.broadcast_to(m, (kb, HEAD_DIM)))
                store(2, rows, jnp.broadcast_to(jnp.sum(p, axis=-1, keepdims=True), (kb, HEAD_DIM)))
            return carry

        lax.fori_loop(0, n_blocks // ATTN_UNROLL, blocks, 0)

    stats = (acc_ref, m_ref, l_ref)
    for g, (_, dil) in enumerate(GROUPS):
        src = qkv_refs[3 * g:3 * g + 3]

        def store_tok(a, rows, val, g=g):
            stats[a][g, rows, :] = val

        if dil <= ATTN_MAX_STRIDE:
            run_group(lambda a, rows, src=src: src[a][rows, :], store_tok, dil, seq)
            continue
        inner = ATTN_MAX_STRIDE
        outer = dil // inner
        assert dil == inner * outer
        seg_len = seq // outer
        for a in range(3):
            for r1 in range(outer):
                for c in range(seg_len // kb):
                    stage_in_ref[a, pl.ds(r1 * seg_len + c * kb, kb), :] = (
                        src[a][strided(r1 + c * kb * outer, kb, outer), :])

        def store_seg(a, rows, val):
            stage_out_ref[a, rows, :] = val

        run_group(lambda a, rows: stage_in_ref[a, rows, :], store_seg, inner, seg_len)
        for a in range(3):
            for r1 in range(outer):
                for c in range(seg_len // kb):
                    stats[a][g, strided(r1 + c * kb * outer, kb, outer), :] = (
                        stage_out_ref[a, pl.ds(r1 * seg_len + c * kb, kb), :])

    def merge(c, carry):
        t0 = pl.multiple_of(c * kb, kb)
        ms = [m_ref[g, pl.ds(t0, kb), :] for g in range(len(GROUPS))]
        mx = functools.reduce(jnp.maximum, ms)
        num = jnp.zeros((kb, HEAD_DIM), _F32)
        den = jnp.zeros((kb, HEAD_DIM), _F32)
        for g in range(len(GROUPS)):
            w = jnp.exp(ms[g] - mx)
            num = num + w * acc_ref[g, pl.ds(t0, kb), :]
            den = den + w * l_ref[g, pl.ds(t0, kb), :]
        o_ref[pl.ds(t0, kb), :] = (num / den).astype(o_ref.dtype)
        return carry

    lax.fori_loop(0, seq // kb, merge, 0)


def attn_prompt(qkv, n_heads):
    bsz, seq, _ = qkv.shape
    ng = len(GROUPS)

    def spec(section):
        return pl.BlockSpec((None, seq, HEAD_DIM), lambda b, h: (b, 0, section * n_heads + h))

    return pl.pallas_call(
        functools.partial(_attn_prompt_kernel, seq=seq),
        out_shape=jax.ShapeDtypeStruct((bsz, seq, n_heads * HEAD_DIM), _BF16),
        grid=(bsz, n_heads),
        in_specs=[spec(s) for s in range(3 * ng)],
        out_specs=pl.BlockSpec((None, seq, HEAD_DIM), lambda b, h: (b, 0, h)),
        scratch_shapes=[pltpu.VMEM((ng, seq, HEAD_DIM), _F32)] * 3 + [pltpu.VMEM((3, seq, HEAD_DIM), _F32)] * 2,
        compiler_params=_params("arbitrary", "arbitrary"),
        name="attn_prompt",
    )(*([qkv] * (3 * ng)))


def _attn_sample_kernel(*refs):
    ng = len(GROUPS)
    q_refs = refs[0:ng]
    knew_refs = refs[ng:2 * ng]
    vnew_refs = refs[2 * ng:3 * ng]
    kc_refs = refs[3 * ng:4 * ng]
    vc_refs = refs[4 * ng:5 * ng]
    o_ref = refs[5 * ng]
    scale = HEAD_DIM ** -0.5
    parts = []
    for g in range(ng):
        q = q_refs[g][...]
        s_new = jnp.sum(q * knew_refs[g][...], axis=-1, keepdims=True) * scale
        s_old = jnp.sum(q[None] * kc_refs[g][...], axis=-1, keepdims=True) * scale
        m = jnp.maximum(jnp.max(s_old, axis=0), s_new)
        p_new = jnp.exp(s_new - m)
        p_old = jnp.exp(s_old - m[None])
        l = p_new + jnp.sum(p_old, axis=0)
        acc = p_new * vnew_refs[g][...] + jnp.sum(p_old * vc_refs[g][...], axis=0)
        parts.append((m, l, acc))
    mx = functools.reduce(jnp.maximum, [p[0] for p in parts])
    num = sum(jnp.exp(m - mx) * acc for m, _, acc in parts)
    den = sum(jnp.exp(m - mx) * l for m, l, _ in parts)
    o_ref[...] = (num / den).astype(o_ref.dtype)


def attn_sample(qkv, caches, n_heads):
    bsz = qkv.shape[0]
    ng = len(GROUPS)
    kb = KEYS_PER_BLOCK

    def new_spec(section):
        return pl.BlockSpec((None, None, n_heads, HEAD_DIM), lambda b: (b, section, 0, 0))

    def cache_spec(kv):
        return pl.BlockSpec((None, kb, None, None, n_heads, HEAD_DIM), lambda b: (b, 0, 0, kv, 0, 0))

    strided = []
    for (win, dil), cache in zip(GROUPS, caches):
        assert cache.shape[1] == win == kb * dil
        strided.append(cache.reshape(bsz, kb, dil, 2, n_heads, HEAD_DIM))
    in_specs = ([new_spec(3 * g) for g in range(ng)] + [new_spec(3 * g + 1) for g in range(ng)]
                + [new_spec(3 * g + 2) for g in range(ng)] + [cache_spec(0)] * ng + [cache_spec(1)] * ng)
    return pl.pallas_call(
        _attn_sample_kernel,
        out_shape=jax.ShapeDtypeStruct((bsz, n_heads, HEAD_DIM), _BF16),
        grid=(bsz,),
        in_specs=in_specs,
        out_specs=pl.BlockSpec((None, n_heads, HEAD_DIM), lambda b: (b, 0, 0)),
        compiler_params=_params("arbitrary"),
        name="attn_sample",
    )(*([qkv] * (3 * ng)), *strided, *strided)


def _cache_roll_kernel(cur_ref, nxt_ref, new_ref, o_ref, *, tl, nblk):
    i = pl.program_id(1)
    o_ref[pl.ds(0, tl - 1)] = cur_ref[pl.ds(1, tl - 1)]

    @pl.when(i < nblk - 1)
    def _():
        o_ref[pl.ds(tl - 1, 1)] = nxt_ref[...]

    @pl.when(i == nblk - 1)
    def _():
        o_ref[pl.ds(tl - 1, 1)] = new_ref[...]


def cache_roll(cache, new_row):
    bsz, length = cache.shape[:2]
    tail = cache.shape[2:]
    tl = min(CACHE_ROLL_ROWS, length)
    assert length % tl == 0
    nblk = length // tl
    zeros = (0,) * len(tail)
    blk = pl.BlockSpec((None, tl) + tail, lambda b, i: (b, i) + zeros)
    nxt = pl.BlockSpec((None, 1) + tail, lambda b, i: (b, jnp.minimum((i + 1) * tl, length - 1)) + zeros)
    new = pl.BlockSpec((None, 1) + tail, lambda b, i: (b, 0) + zeros)
    return pl.pallas_call(
        functools.partial(_cache_roll_kernel, tl=tl, nblk=nblk),
        out_shape=jax.ShapeDtypeStruct(cache.shape, cache.dtype),
        grid=(bsz, nblk),
        in_specs=[blk, nxt, new],
        out_specs=blk,
        compiler_params=_params("arbitrary", "arbitrary"),
        name="cache_roll",
    )(cache, cache, new_row)


def _trunk(x, weights, sample_state):
    (w_in, b_in, conv_w, conv_b, w_ra, b_ra, w_ix, b_ix, lam, w_out, b_out, w_qkv, w_o,
     w1, b1, w2, b2, ln_g, ln_b) = weights
    bsz, seq, d = x.shape
    n_heads = d // HEAD_DIM
    is_sample = sample_state is not None
    rows = bsz * seq
    x2d = x.reshape(rows, d)
    if is_sample:
        assert seq == 1
        pad_rows = -rows % V7X_BF16_SUBLANES
        x2d = jnp.pad(x2d, ((0, pad_rows), (0, 0)))
    m = x2d.shape[0]
    xbf = x2d.astype(_BF16)

    def mlp(xf, xb, i):
        h = matmul(xb, w1, i, b1[i], act="relu2", out_dtype=_BF16)
        y = matmul(h, w2, i, b2[i])
        return deepnorm(xf, y, ln_g[i][1], ln_b[i][1])

    d_rnn = conv_w[0].shape[1]
    u = matmul(xbf, w_in, 0, b_in[0], act="gelu", act_cols=d_rnn)
    lru_args = (conv_w[0], conv_b[0], w_ra[0], b_ra[0], w_ix[0], b_ix[0], lam[0])
    if is_sample:
        state_conv, state_h, caches = sample_state
        padr = lambda a: jnp.pad(a, ((0, m - rows), (0, 0)))
        conv_prev = [padr(state_conv[0][:, k]) for k in range(CONV_W - 1)]
        hg, h_new = rglru_step(u, conv_prev, padr(state_h[0]), *lru_args)
        new_conv = jnp.concatenate([state_conv[0][:, 1:], u[:rows, None, d_rnn:]], axis=1)
        h_last = h_new[:rows]
    else:
        u3 = u.reshape(bsz, seq, 2 * d_rnn)
        hg, h_last = rglru_prompt(u3, *lru_args)
        hg = hg.reshape(m, d_rnn)
        h_last = h_last.reshape(bsz, d_rnn)
        new_conv = u3[:, seq - (CONV_W - 1):, d_rnn:]
    y = matmul(hg, w_out, 0, b_out[0])
    xf, xb16 = deepnorm(x2d, y, ln_g[0][0], ln_b[0][0])
    xf, xb16 = mlp(xf, xb16, 0)

    ng = len(GROUPS)
    zero_bias = jnp.zeros((1, w_qkv.shape[2]), _F32)
    qkv = matmul(xb16, w_qkv, 0, zero_bias)
    kv_width = 2 * n_heads * HEAD_DIM
    sec = n_heads * HEAD_DIM
    if is_sample:
        qkv_s = qkv[:rows].reshape(bsz, 3 * ng, n_heads, HEAD_DIM)
        o = attn_sample(qkv_s, [c[0] for c in caches], n_heads).reshape(rows, d)
        o = jnp.pad(o, ((0, m - rows), (0, 0)))
        new_kv = []
        for g in range(ng):
            new_row = qkv_s[:, 3 * g + 1:3 * g + 3].reshape(bsz, 1, 2, n_heads, HEAD_DIM)
            new_kv.append(cache_roll(caches[g][0], new_row)[None])
    else:
        o = attn_prompt(qkv.reshape(bsz, seq, -1), n_heads).reshape(m, d)
        new_kv = []
        for g, (win, _) in enumerate(GROUPS):
            keep = min(win, seq)
            kv = qkv.reshape(bsz, seq, -1)[:, seq - keep:, 3 * g * sec + sec:3 * g * sec + sec + kv_width]
            new_kv.append(kv.reshape(1, bsz, keep, 2, n_heads, HEAD_DIM))
    y = matmul(o, w_o, 0, jnp.zeros((1, d), _F32))
    xf, xb16 = deepnorm(xf, y, ln_g[1][0], ln_b[1][0])
    xf, xb16 = mlp(xf, xb16, 1)

    out = xf[:rows].reshape(bsz, seq, d)
    return out, new_conv[None], h_last[None], new_kv


def kernel(x_prompt, x_sample, state_conv, state_h, cache_kv_w128, cache_kv_w512, cache_kv_w2048,
           lru_w_in, lru_b_in, lru_conv_w, lru_conv_b, lru_w_ra, lru_b_ra, lru_w_ix, lru_b_ix,
           lru_lambda, lru_w_out, lru_b_out, attn_w_qkv, attn_w_o,
           mlp_w1, mlp_b1, mlp_w2, mlp_b2, ln_g, ln_b):
    row = lambda v: v[:, None, :]
    weights = (lru_w_in, row(lru_b_in), lru_conv_w, row(lru_conv_b), lru_w_ra, row(lru_b_ra),
               lru_w_ix, row(lru_b_ix), row(lru_lambda), lru_w_out, row(lru_b_out),
               attn_w_qkv, attn_w_o, mlp_w1, row(mlp_b1), mlp_w2, row(mlp_b2),
               ln_g[:, :, None, :], ln_b[:, :, None, :])
    y_p, conv_p, h_p, kv_p = _trunk(x_prompt, weights, None)
    y_s, conv_s, h_s, kv_s = _trunk(x_sample, weights,
                                    (state_conv, state_h, (cache_kv_w128, cache_kv_w512, cache_kv_w2048)))
    return (y_p, y_s, conv_p, h_p, kv_p[0], kv_p[1], kv_p[2], conv_s, h_s, kv_s[0], kv_s[1], kv_s[2])
```

```python
import functools

import jax
import jax.numpy as jnp
from jax import lax
from jax.experimental import pallas as pl
from jax.experimental.pallas import tpu as pltpu

HEAD_DIM = 128
CONV_W = 4
LRU_C = 8.0
GROUPS = ((128, 1), (512, 4), (2048, 16))
KEYS_PER_BLOCK = 128
LN_EPS = 1e-5
DEPTH = 2
ALPHA = (2.0 * DEPTH) ** 0.25

V7X_SUBLANES = 8
V7X_BF16_SUBLANES = 16
V7X_VMEM_LIMIT_BYTES = 62 * 1024 * 1024

MM_TILE = 1024
MM_SMALL_M_TILE_N = 512
MM_FULL_K = 4096
MM_K_TILE = 1024
MM_KACC_TILE_M = 2048
ATTN_UNROLL = 8
ATTN_MAX_STRIDE = 4
SIDE_ROLL_MIN_ROWS = 32
LN_ROWS = 256
SCAN_CHUNK = 256
SCAN_SUB = 64

_BF16 = jnp.bfloat16
_F32 = jnp.float32


def _params(*sem):
    return pltpu.CompilerParams(dimension_semantics=sem, vmem_limit_bytes=V7X_VMEM_LIMIT_BYTES)


def _apply_act(acc, act):
    if act == "gelu":
        return jax.nn.gelu(acc)
    if act == "relu2":
        r = jnp.maximum(acc, 0.0)
        return r * r
    return acc


def _mm_kernel(*refs, act, act_tiles, ck, nj, ni, side_blocks):
    ns = len(side_blocks)
    x_ref, wc_ref, b_ref = refs[:3]
    side_in = refs[3:3 + 2 * ns]
    o_ref = refs[3 + 2 * ns]
    side_out = refs[4 + 2 * ns:4 + 3 * ns]
    wbf_ref = refs[4 + 3 * ns]
    j = pl.program_id(0)
    i = pl.program_id(1)

    for c, (tl, n_side) in enumerate(side_blocks):
        @pl.when(j * ni + i < n_side)
        def _(c=c, tl=tl):
            cur_ref, nxt_ref = side_in[2 * c:2 * c + 2]
            side_out[c][pl.ds(0, tl - 1)] = cur_ref[pl.ds(1, tl - 1)]
            side_out[c][pl.ds(tl - 1, 1)] = nxt_ref[...]

    @pl.when(j < nj)
    def _():
        r0 = pl.multiple_of(i * ck, ck)
        wbf_ref[j % 2, pl.ds(r0, ck), :] = wc_ref[...].astype(_BF16)

    @pl.when(j > 0)
    def _():
        acc = jnp.dot(x_ref[...], wbf_ref[(j - 1) % 2], preferred_element_type=_F32) + b_ref[...]
        if act_tiles is None:
            o_ref[...] = _apply_act(acc, act).astype(o_ref.dtype)
        else:
            @pl.when(j - 1 < act_tiles)
            def _():
                o_ref[...] = _apply_act(acc, act).astype(o_ref.dtype)

            @pl.when(j - 1 >= act_tiles)
            def _():
                o_ref[...] = acc.astype(o_ref.dtype)


def _mm_acc_kernel(x_ref, w_ref, b_ref, o_ref, acc_ref, *, act, nk):
    k = pl.program_id(2)

    @pl.when(k == 0)
    def _():
        acc_ref[...] = jnp.zeros_like(acc_ref)

    acc_ref[...] += jnp.dot(x_ref[...], w_ref[...].astype(_BF16), preferred_element_type=_F32)

    @pl.when(k == nk - 1)
    def _():
        o_ref[...] = _apply_act(acc_ref[...] + b_ref[...], act).astype(o_ref.dtype)


def _roll_rows(cache, steps):
    bsz, length = cache.shape[:2]
    tl = min(SIDE_ROLL_MIN_ROWS, length)
    while bsz * (length // tl) > steps:
        assert tl < length
        tl *= 2
    return tl


def _side_roll_specs(cache, tl, ni):
    bsz, length = cache.shape[:2]
    tail = cache.shape[2:]
    assert length % tl == 0
    per_batch = length // tl
    n_side = bsz * per_batch
    zeros = (0,) * len(tail)

    def where(j, i):
        t = jnp.minimum(j * ni + i, n_side - 1)
        return t // per_batch, t % per_batch

    def cur_map(j, i):
        return where(j, i) + zeros

    def nxt_map(j, i):
        bi, ti = where(j, i)
        return (bi, jnp.minimum((ti + 1) * tl, length - 1)) + zeros

    cur = pl.BlockSpec((None, tl) + tail, cur_map)
    nxt = pl.BlockSpec((None, 1) + tail, nxt_map)
    return cur, nxt, n_side


def matmul(x, w, layer, b, *, act=None, act_cols=None, out_dtype=_F32, side_rolls=()):
    m, kdim = x.shape
    n = w.shape[2]
    if kdim <= MM_FULL_K:
        tm = min(MM_TILE, m)
        tn = min(MM_TILE if m >= MM_TILE else MM_SMALL_M_TILE_N, n)
        assert m % tm == 0 and n % tn == 0
        ni, nj = m // tm, n // tn
        assert kdim % ni == 0
        ck = kdim // ni
        act_tiles = None
        if act_cols is not None:
            assert act_cols % tn == 0
            act_tiles = act_cols // tn
        side_in_specs, side_out_specs, side_blocks, side_args = [], [], [], []
        for cache in side_rolls:
            tl = _roll_rows(cache, (nj + 1) * ni)
            cur, nxt, n_side = _side_roll_specs(cache, tl, ni)
            side_in_specs += [cur, nxt]
            side_out_specs.append(cur)
            side_blocks.append((tl, n_side))
            side_args += [cache, cache]
        outs = pl.pallas_call(
            functools.partial(_mm_kernel, act=act, act_tiles=act_tiles, ck=ck, nj=nj, ni=ni,
                              side_blocks=tuple(side_blocks)),
            out_shape=(jax.ShapeDtypeStruct((m, n), out_dtype),
                       *[jax.ShapeDtypeStruct(c.shape, c.dtype) for c in side_rolls]),
            grid=(nj + 1, ni),
            in_specs=[
                pl.BlockSpec((tm, kdim), lambda j, i: (jnp.where(j == 0, 0, i), 0)),
                pl.BlockSpec((None, ck, tn),
                             lambda j, i: (layer, jnp.where(j < nj, i, ni - 1), jnp.minimum(j, nj - 1))),
                pl.BlockSpec((1, tn), lambda j, i: (0, jnp.maximum(j - 1, 0))),
                *side_in_specs,
            ],
            out_specs=(pl.BlockSpec((tm, tn), lambda j, i: (jnp.where(j == 0, 0, i), jnp.maximum(j - 1, 0))),
                       *side_out_specs),
            scratch_shapes=[pltpu.VMEM((2, kdim, tn), _BF16)],
            compiler_params=_params("arbitrary", "arbitrary"),
            name="matmul",
        )(x, w, b, *side_args)
        return outs if side_rolls else outs[0]
    assert act_cols is None and not side_rolls
    tm = min(MM_KACC_TILE_M, m)
    tn = min(MM_TILE, n)
    tk = MM_K_TILE
    assert m % tm == 0 and n % tn == 0 and kdim % tk == 0
    nk = kdim // tk
    return pl.pallas_call(
        functools.partial(_mm_acc_kernel, act=act, nk=nk),
        out_shape=jax.ShapeDtypeStruct((m, n), out_dtype),
        grid=(n // tn, m // tm, nk),
        in_specs=[
            pl.BlockSpec((tm, tk), lambda j, i, k: (i, k)),
            pl.BlockSpec((None, tk, tn), lambda j, i, k: (layer, k, j)),
            pl.BlockSpec((1, tn), lambda j, i, k: (0, j)),
        ],
        out_specs=pl.BlockSpec((tm, tn), lambda j, i, k: (i, j)),
        scratch_shapes=[pltpu.VMEM((tm, tn), _F32)],
        compiler_params=_params("arbitrary", "arbitrary", "arbitrary"),
        name="matmul_kacc",
    )(x, w, b)


def _ln_kernel(x_ref, y_ref, g_ref, b_ref, o_ref, obf_ref):
    z = ALPHA * x_ref[...] + y_ref[...]
    mu = jnp.mean(z, axis=-1, keepdims=True)
    zc = z - mu
    var = jnp.mean(zc * zc, axis=-1, keepdims=True)
    out = zc * lax.rsqrt(var + LN_EPS) * g_ref[...] + b_ref[...]
    o_ref[...] = out
    obf_ref[...] = out.astype(_BF16)


def deepnorm(x, y, g, b):
    m, d = x.shape
    tm = min(LN_ROWS, m)
    assert m % tm == 0
    row = pl.BlockSpec((tm, d), lambda i: (i, 0))
    vec = pl.BlockSpec((1, d), lambda i: (0, 0))
    return pl.pallas_call(
        _ln_kernel,
        out_shape=(jax.ShapeDtypeStruct((m, d), _F32), jax.ShapeDtypeStruct((m, d), _BF16)),
        grid=(m // tm,),
        in_specs=[row, row, vec, vec],
        out_specs=(row, row),
        compiler_params=_params("arbitrary"),
        name="deepnorm",
    )(x, y, g, b)


def _softplus(x):
    return jnp.maximum(x, 0.0) + jnp.log1p(jnp.exp(-jnp.abs(x)))


def _lru_coeffs(xc, wra, bra, wix, bix, sp):
    xcb = xc.astype(_BF16)
    r = jax.nn.sigmoid(jnp.dot(xcb, wra, preferred_element_type=_F32) + bra)
    i = jax.nn.sigmoid(jnp.dot(xcb, wix, preferred_element_type=_F32) + bix)
    log_a = -LRU_C * r * sp
    a = jnp.exp(log_a)
    t = jnp.tanh(log_a)
    bt = jnp.sqrt(-2.0 * t / (1.0 - t)) * (i * xc)
    return a, bt


def _shift_rows(x, s, fill):
    rows = lax.broadcasted_iota(jnp.int32, x.shape, 0)
    return jnp.where(rows < s, fill, pltpu.roll(x, s, 0))


def _rglru_kernel(xb_ref, gate_ref, cw_ref, cb_ref, wra_ref, bra_ref, wix_ref, bix_ref, lam_ref,
                  o_ref, hlast_ref, xpad_ref, *, seq, chunk):
    pad = V7X_SUBLANES
    cw = cw_ref[...]
    cbias = cb_ref[...]
    wra = wra_ref[...].astype(_BF16)
    wix = wix_ref[...].astype(_BF16)
    bra = bra_ref[...]
    bix = bix_ref[...]
    sp = _softplus(-lam_ref[...])

    xpad_ref[pl.ds(0, pad), :] = jnp.zeros((pad, xpad_ref.shape[1]), _F32)
    xpad_ref[pl.ds(pad, seq), :] = xb_ref[...]

    def body(c, h):
        t0 = pl.multiple_of(c * chunk, chunk)
        xw = xpad_ref[pl.ds(t0, chunk + pad), :]
        xc = cbias
        for k in range(CONV_W):
            off = pad - (CONV_W - 1) + k
            xc = xc + cw[k:k + 1, :] * xw[off:off + chunk, :]
        a, bt = _lru_coeffs(xc, wra, bra, wix, bix, sp)
        hs = []
        for q in range(chunk // SCAN_SUB):
            aq = a[q * SCAN_SUB:(q + 1) * SCAN_SUB]
            bq = bt[q * SCAN_SUB:(q + 1) * SCAN_SUB]
            s = 1
            while s < SCAN_SUB:
                if s < V7X_SUBLANES:
                    a_sh = _shift_rows(aq, s, 1.0)
                    b_sh = _shift_rows(bq, s, 0.0)
                    bq = aq * b_sh + bq
                    aq = aq * a_sh
                else:
                    bq = jnp.concatenate([bq[:s], aq[s:] * bq[:-s] + bq[s:]], axis=0)
                    aq = jnp.concatenate([aq[:s], aq[s:] * aq[:-s]], axis=0)
                s *= 2
            hq = aq * h + bq
            h = hq[SCAN_SUB - 1:SCAN_SUB, :]
            hs.append(hq)
        hs = jnp.concatenate(hs, axis=0)
        o_ref[pl.ds(t0, chunk), :] = (hs * gate_ref[pl.ds(t0, chunk), :]).astype(o_ref.dtype)
        return h

    h0 = jnp.zeros((1, xb_ref.shape[1]), _F32)
    hlast_ref[...] = lax.fori_loop(0, seq // chunk, body, h0)


def rglru_prompt(u, cw, cb, wra, bra, wix, bix, lam):
    bsz, seq, c2 = u.shape
    c = c2 // 2
    nblk, blk, _ = wra.shape
    assert c == nblk * blk and seq % SCAN_CHUNK == 0
    col = lambda b, n: (b, 0, n)
    vec = pl.BlockSpec((1, blk), lambda b, n: (0, n))
    wspec = pl.BlockSpec((None, blk, blk), lambda b, n: (n, 0, 0))
    return pl.pallas_call(
        functools.partial(_rglru_kernel, seq=seq, chunk=SCAN_CHUNK),
        out_shape=(jax.ShapeDtypeStruct((bsz, seq, c), _BF16), jax.ShapeDtypeStruct((bsz, 1, c), _F32)),
        grid=(bsz, nblk),
        in_specs=[
            pl.BlockSpec((None, seq, blk), lambda b, n: (b, 0, n + nblk)),
            pl.BlockSpec((None, seq, blk), col),
            pl.BlockSpec((CONV_W, blk), lambda b, n: (0, n)),
            vec, wspec, vec, wspec, vec, vec,
        ],
        out_specs=(pl.BlockSpec((None, seq, blk), col), pl.BlockSpec((None, 1, blk), col)),
        scratch_shapes=[pltpu.VMEM((seq + V7X_SUBLANES, blk), _F32)],
        compiler_params=_params("arbitrary", "arbitrary"),
        name="rglru_prompt",
    )(u, u, cw, cb, wra, bra, wix, bix, lam)


def _rglru_step_kernel(xb_ref, gate_ref, c0_ref, c1_ref, c2_ref, h_ref, cw_ref, cb_ref,
                       wra_ref, bra_ref, wix_ref, bix_ref, lam_ref, o_ref, hnew_ref):
    cw = cw_ref[...]
    xc = (cb_ref[...] + cw[0:1, :] * c0_ref[...] + cw[1:2, :] * c1_ref[...]
          + cw[2:3, :] * c2_ref[...] + cw[3:4, :] * xb_ref[...])
    a, bt = _lru_coeffs(xc, wra_ref[...].astype(_BF16), bra_ref[...], wix_ref[...].astype(_BF16),
                        bix_ref[...], _softplus(-lam_ref[...]))
    h = a * h_ref[...] + bt
    hnew_ref[...] = h
    o_ref[...] = (h * gate_ref[...]).astype(o_ref.dtype)


def rglru_step(u, conv_prev, h_prev, cw, cb, wra, bra, wix, bix, lam):
    rows, c2 = u.shape
    c = c2 // 2
    nblk, blk, _ = wra.shape
    col = pl.BlockSpec((rows, blk), lambda n: (0, n))
    vec = pl.BlockSpec((1, blk), lambda n: (0, n))
    wspec = pl.BlockSpec((None, blk, blk), lambda n: (n, 0, 0))
    return pl.pallas_call(
        _rglru_step_kernel,
        out_shape=(jax.ShapeDtypeStruct((rows, c), _BF16), jax.ShapeDtypeStruct((rows, c), _F32)),
        grid=(nblk,),
        in_specs=[pl.BlockSpec((rows, blk), lambda n: (0, n + nblk)), col, col, col, col, col,
                  pl.BlockSpec((CONV_W, blk), lambda n: (0, n)), vec, wspec, vec, wspec, vec, vec],
        out_specs=(col, col),
        compiler_params=_params("arbitrary"),
        name="rglru_step",
    )(u, u, *conv_prev, h_prev, cw, cb, wra, bra, wix, bix, lam)


def _attn_prompt_kernel(*refs, seq):
    qkv_refs = refs[:9]
    o_ref = refs[9]
    acc_ref, m_ref, l_ref, stage_in_ref, stage_out_ref = refs[10:15]
    kb = KEYS_PER_BLOCK
    scale = HEAD_DIM ** -0.5
    qi = lax.broadcasted_iota(jnp.int32, (kb, kb), 0)
    ki = lax.broadcasted_iota(jnp.int32, (kb, kb), 1)
    cur_mask = ki <= qi
    qi2 = lax.broadcasted_iota(jnp.int32, (kb, 2 * kb), 0)
    ki2 = lax.broadcasted_iota(jnp.int32, (kb, 2 * kb), 1)
    cur_part = jnp.logical_and(ki2 >= kb, ki2 - kb <= qi2)
    prev_part = jnp.logical_and(ki2 < kb, ki2 >= qi2)
    n_blocks = seq // kb
    assert n_blocks % ATTN_UNROLL == 0

    def strided(start, n, st):
        return pl.ds(start, n) if st == 1 else pl.ds(start, n, stride=st)

    def run_group(load, store, st, seg_len):
        per_res = seg_len // st
        nblk = per_res // kb
        assert per_res % kb == 0 and n_blocks % (st * nblk) == 0

        def block_rows(idx):
            seg = idx // (st * nblk)
            r = (idx // nblk) % st
            j = idx % nblk
            first = seg * seg_len + r
            start = first + j * (kb * st)
            prows = strided(jnp.maximum(start - kb * st, first), kb, st)
            return strided(start, kb, st), prows, j

        def scores(rows, prows, j):
            q = load(0, rows).astype(_BF16)
            k = load(1, rows).astype(_BF16)
            if nblk > 1:
                k = jnp.concatenate([load(1, prows).astype(_BF16), k], axis=0)
                mask = jnp.logical_or(cur_part, jnp.logical_and(prev_part, j > 0))
            else:
                mask = cur_mask
            s = lax.dot_general(q, k, (((1,), (1,)), ((), ())), preferred_element_type=_F32) * scale
            return jnp.where(mask, s, -jnp.inf)

        def values(rows, prows):
            v = load(2, rows).astype(_BF16)
            if nblk > 1:
                v = jnp.concatenate([load(2, prows).astype(_BF16), v], axis=0)
            return v

        def blocks(it, carry):
            where = [block_rows(it * ATTN_UNROLL + u) for u in range(ATTN_UNROLL)]
            ss = [scores(rows, prows, j) for rows, prows, j in where]
            ms = [jnp.max(s, axis=-1, keepdims=True) for s in ss]
            ps = [jnp.exp(s - m) for s, m in zip(ss, ms)]
            for (rows, prows, _), m, p in zip(where, ms, ps):
                store(0, rows, jnp.dot(p.astype(_BF16), values(rows, prows), preferred_element_type=_F32))
                store(1, rows, jnp.broadcast_to(m, (kb, HEAD_DIM)))
                store(2, rows, jnp.broadcast_to(jnp.sum(p, axis=-1, keepdims=True), (kb, HEAD_DIM)))
            return carry

        lax.fori_loop(0, n_blocks // ATTN_UNROLL, blocks, 0)

    stats = (acc_ref, m_ref, l_ref)
    for g, (_, dil) in enumerate(GROUPS):
        src = qkv_refs[3 * g:3 * g + 3]

        def store_tok(a, rows, val, g=g):
            stats[a][g, rows, :] = val

        if dil <= ATTN_MAX_STRIDE:
            run_group(lambda a, rows, src=src: src[a][rows, :], store_tok, dil, seq)
            continue
        inner = ATTN_MAX_STRIDE
        outer = dil // inner
        assert dil == inner * outer
        seg_len = seq // outer
        for a in range(3):
            for r1 in range(outer):
                for c in range(seg_len // kb):
                    stage_in_ref[a, pl.ds(r1 * seg_len + c * kb, kb), :] = (
                        src[a][strided(r1 + c * kb * outer, kb, outer), :])

        def store_seg(a, rows, val):
            stage_out_ref[a, rows, :] = val

        run_group(lambda a, rows: stage_in_ref[a, rows, :], store_seg, inner, seg_len)
        for a in range(3):
            for r1 in range(outer):
                for c in range(seg_len // kb):
                    stats[a][g, strided(r1 + c * kb * outer, kb, outer), :] = (
                        stage_out_ref[a, pl.ds(r1 * seg_len + c * kb, kb), :])

    def merge(c, carry):
        t0 = pl.multiple_of(c * kb, kb)
        ms = [m_ref[g, pl.ds(t0, kb), :] for g in range(len(GROUPS))]
        mx = functools.reduce(jnp.maximum, ms)
        num = jnp.zeros((kb, HEAD_DIM), _F32)
        den = jnp.zeros((kb, HEAD_DIM), _F32)
        for g in range(len(GROUPS)):
            w = jnp.exp(ms[g] - mx)
            num = num + w * acc_ref[g, pl.ds(t0, kb), :]
            den = den + w * l_ref[g, pl.ds(t0, kb), :]
        o_ref[pl.ds(t0, kb), :] = (num / den).astype(o_ref.dtype)
        return carry

    lax.fori_loop(0, seq // kb, merge, 0)


def attn_prompt(qkv, n_heads):
    bsz, seq, _ = qkv.shape
    ng = len(GROUPS)

    def spec(section):
        return pl.BlockSpec((None, seq, HEAD_DIM), lambda b, h: (b, 0, section * n_heads + h))

    return pl.pallas_call(
        functools.partial(_attn_prompt_kernel, seq=seq),
        out_shape=jax.ShapeDtypeStruct((bsz, seq, n_heads * HEAD_DIM), _BF16),
        grid=(bsz, n_heads),
        in_specs=[spec(s) for s in range(3 * ng)],
        out_specs=pl.BlockSpec((None, seq, HEAD_DIM), lambda b, h: (b, 0, h)),
        scratch_shapes=[pltpu.VMEM((ng, seq, HEAD_DIM), _F32)] * 3 + [pltpu.VMEM((3, seq, HEAD_DIM), _F32)] * 2,
        compiler_params=_params("arbitrary", "arbitrary"),
        name="attn_prompt",
    )(*([qkv] * (3 * ng)))


def _attn_sample_kernel(*refs):
    ng = len(GROUPS)
    q_refs = refs[0:ng]
    knew_refs = refs[ng:2 * ng]
    vnew_refs = refs[2 * ng:3 * ng]
    kc_refs = refs[3 * ng:4 * ng]
    vc_refs = refs[4 * ng:5 * ng]
    o_ref = refs[5 * ng]
    scale = HEAD_DIM ** -0.5
    parts = []
    for g in range(ng):
        q = q_refs[g][...]
        s_new = jnp.sum(q * knew_refs[g][...], axis=-1, keepdims=True) * scale
        s_old = jnp.sum(q[None] * kc_refs[g][...], axis=-1, keepdims=True) * scale
        m = jnp.maximum(jnp.max(s_old, axis=0), s_new)
        p_new = jnp.exp(s_new - m)
        p_old = jnp.exp(s_old - m[None])
        l = p_new + jnp.sum(p_old, axis=0)
        acc = p_new * vnew_refs[g][...] + jnp.sum(p_old * vc_refs[g][...], axis=0)
        parts.append((m, l, acc))
    mx = functools.reduce(jnp.maximum, [p[0] for p in parts])
    num = sum(jnp.exp(m - mx) * acc for m, _, acc in parts)
    den = sum(jnp.exp(m - mx) * l for m, l, _ in parts)
    o_ref[...] = (num / den).astype(o_ref.dtype)


def attn_sample(qkv, caches, n_heads):
    bsz = qkv.shape[0]
    ng = len(GROUPS)
    kb = KEYS_PER_BLOCK

    def new_spec(section):
        return pl.BlockSpec((None, None, n_heads, HEAD_DIM), lambda b: (b, section, 0, 0))

    def cache_spec(kv):
        return pl.BlockSpec((None, kb, None, None, n_heads, HEAD_DIM), lambda b: (b, 0, 0, kv, 0, 0))

    strided = []
    for (win, dil), cache in zip(GROUPS, caches):
        assert cache.shape[1] == win == kb * dil
        strided.append(cache.reshape(bsz, kb, dil, 2, n_heads, HEAD_DIM))
    in_specs = ([new_spec(3 * g) for g in range(ng)] + [new_spec(3 * g + 1) for g in range(ng)]
                + [new_spec(3 * g + 2) for g in range(ng)] + [cache_spec(0)] * ng + [cache_spec(1)] * ng)
    return pl.pallas_call(
        _attn_sample_kernel,
        out_shape=jax.ShapeDtypeStruct((bsz, n_heads, HEAD_DIM), _BF16),
        grid=(bsz,),
        in_specs=in_specs,
        out_specs=pl.BlockSpec((None, n_heads, HEAD_DIM), lambda b: (b, 0, 0)),
        compiler_params=_params("arbitrary"),
        name="attn_sample",
    )(*([qkv] * (3 * ng)), *strided, *strided)


def _set_last_kernel(rolled_ref, new_ref, o_ref):
    del rolled_ref
    o_ref[...] = new_ref[...]


def cache_set_last(rolled, new_row):
    bsz, length = rolled.shape[:2]
    tail = rolled.shape[2:]
    zeros = (0,) * len(tail)
    return pl.pallas_call(
        _set_last_kernel,
        out_shape=jax.ShapeDtypeStruct(rolled.shape, rolled.dtype),
        grid=(bsz,),
        in_specs=[pl.BlockSpec(memory_space=pl.ANY),
                  pl.BlockSpec((None, 1) + tail, lambda b: (b, 0) + zeros)],
        out_specs=pl.BlockSpec((None, 1) + tail, lambda b: (b, length - 1) + zeros),
        input_output_aliases={0: 0},
        compiler_params=_params("arbitrary"),
        name="cache_set_last",
    )(rolled, new_row)


def _trunk(x, weights, sample_state, ride_caches=()):
    (w_in, b_in, conv_w, conv_b, w_ra, b_ra, w_ix, b_ix, lam, w_out, b_out, w_qkv, w_o,
     w1, b1, w2, b2, ln_g, ln_b) = weights
    bsz, seq, d = x.shape
    n_heads = d // HEAD_DIM
    is_sample = sample_state is not None
    rows = bsz * seq
    x2d = x.reshape(rows, d)
    if is_sample:
        assert seq == 1
        pad_rows = -rows % V7X_BF16_SUBLANES
        x2d = jnp.pad(x2d, ((0, pad_rows), (0, 0)))
    m = x2d.shape[0]
    xbf = x2d.astype(_BF16)

    rolled_out = []

    def mlp(xf, xb, i, side_rolls=()):
        h = matmul(xb, w1, i, b1[i], act="relu2", out_dtype=_BF16, side_rolls=side_rolls)
        if side_rolls:
            rolled_out.extend(h[1:])
            h = h[0]
        y = matmul(h, w2, i, b2[i])
        return deepnorm(xf, y, ln_g[i][1], ln_b[i][1])

    d_rnn = conv_w[0].shape[1]
    u = matmul(xbf, w_in, 0, b_in[0], act="gelu", act_cols=d_rnn)
    lru_args = (conv_w[0], conv_b[0], w_ra[0], b_ra[0], w_ix[0], b_ix[0], lam[0])
    if is_sample:
        state_conv, state_h, caches, rolled = sample_state
        padr = lambda a: jnp.pad(a, ((0, m - rows), (0, 0)))
        conv_prev = [padr(state_conv[0][:, k]) for k in range(CONV_W - 1)]
        hg, h_new = rglru_step(u, conv_prev, padr(state_h[0]), *lru_args)
        new_conv = jnp.concatenate([state_conv[0][:, 1:], u[:rows, None, d_rnn:]], axis=1)
        h_last = h_new[:rows]
    else:
        u3 = u.reshape(bsz, seq, 2 * d_rnn)
        hg, h_last = rglru_prompt(u3, *lru_args)
        hg = hg.reshape(m, d_rnn)
        h_last = h_last.reshape(bsz, d_rnn)
        new_conv = u3[:, seq - (CONV_W - 1):, d_rnn:]
    y = matmul(hg, w_out, 0, b_out[0])
    xf, xb16 = deepnorm(x2d, y, ln_g[0][0], ln_b[0][0])
    xf, xb16 = mlp(xf, xb16, 0)

    ng = len(GROUPS)
    zero_bias = jnp.zeros((1, w_qkv.shape[2]), _F32)
    qkv = matmul(xb16, w_qkv, 0, zero_bias, side_rolls=ride_caches[-1:])
    if ride_caches:
        qkv, big_rolled = qkv
    kv_width = 2 * n_heads * HEAD_DIM
    sec = n_heads * HEAD_DIM
    if is_sample:
        qkv_s = qkv[:rows].reshape(bsz, 3 * ng, n_heads, HEAD_DIM)
        o = attn_sample(qkv_s, caches, n_heads).reshape(rows, d)
        o = jnp.pad(o, ((0, m - rows), (0, 0)))
        new_kv = []
        for g in range(ng):
            new_row = qkv_s[:, 3 * g + 1:3 * g + 3].reshape(bsz, 1, 2, n_heads, HEAD_DIM)
            new_kv.append(cache_set_last(rolled[g], new_row)[None])
    else:
        o = attn_prompt(qkv.reshape(bsz, seq, -1), n_heads).reshape(m, d)
        new_kv = []
        for g, (win, _) in enumerate(GROUPS):
            keep = min(win, seq)
            kv = qkv.reshape(bsz, seq, -1)[:, seq - keep:, 3 * g * sec + sec:3 * g * sec + sec + kv_width]
            new_kv.append(kv.reshape(1, bsz, keep, 2, n_heads, HEAD_DIM))
    y = matmul(o, w_o, 0, jnp.zeros((1, d), _F32))
    xf, xb16 = deepnorm(xf, y, ln_g[1][0], ln_b[1][0])
    xf, xb16 = mlp(xf, xb16, 1, side_rolls=ride_caches[:-1])
    if ride_caches:
        rolled_out.append(big_rolled)

    out = xf[:rows].reshape(bsz, seq, d)
    return out, new_conv[None], h_last[None], new_kv, rolled_out


def kernel(x_prompt, x_sample, state_conv, state_h, cache_kv_w128, cache_kv_w512, cache_kv_w2048,
           lru_w_in, lru_b_in, lru_conv_w, lru_conv_b, lru_w_ra, lru_b_ra, lru_w_ix, lru_b_ix,
           lru_lambda, lru_w_out, lru_b_out, attn_w_qkv, attn_w_o,
           mlp_w1, mlp_b1, mlp_w2, mlp_b2, ln_g, ln_b):
    row = lambda v: v[:, None, :]
    weights = (lru_w_in, row(lru_b_in), lru_conv_w, row(lru_conv_b), lru_w_ra, row(lru_b_ra),
               lru_w_ix, row(lru_b_ix), row(lru_lambda), lru_w_out, row(lru_b_out),
               attn_w_qkv, attn_w_o, mlp_w1, row(mlp_b1), mlp_w2, row(mlp_b2),
               ln_g[:, :, None, :], ln_b[:, :, None, :])
    caches = (cache_kv_w128[0], cache_kv_w512[0], cache_kv_w2048[0])
    y_p, conv_p, h_p, kv_p, rolled = _trunk(x_prompt, weights, None, ride_caches=caches)
    y_s, conv_s, h_s, kv_s, _ = _trunk(x_sample, weights, (state_conv, state_h, caches, rolled))
    return (y_p, y_s, conv_p, h_p, kv_p[0], kv_p[1], kv_p[2], conv_s, h_s, kv_s[0], kv_s[1], kv_s[2])
```

```python
import functools

import jax
import jax.numpy as jnp
from jax import lax
from jax.experimental import pallas as pl
from jax.experimental.pallas import tpu as pltpu

HEAD_DIM = 128
CONV_W = 4
LRU_C = 8.0
GROUPS = ((128, 1), (512, 4), (2048, 16))
KEYS_PER_BLOCK = 128
LN_EPS = 1e-5
DEPTH = 2
ALPHA = (2.0 * DEPTH) ** 0.25

V7X_SUBLANES = 8
V7X_BF16_SUBLANES = 16
V7X_VMEM_LIMIT_BYTES = 62 * 1024 * 1024

MM_TILE = 1024
MM_FULL_K = 4096
MM_KACC_TILE_M = 2048
MM_KACC_SUBTILES = 1
ATTN_UNROLL = 8
ATTN_MAX_STRIDE = 4
SIDE_ROLL_MIN_ROWS = 32
LN_ROWS = 256
SCAN_CHUNK = 256
SCAN_SUB = 64

_BF16 = jnp.bfloat16
_F32 = jnp.float32


def _params(*sem):
    return pltpu.CompilerParams(dimension_semantics=sem, vmem_limit_bytes=V7X_VMEM_LIMIT_BYTES)


def _apply_act(acc, act):
    if act == "gelu":
        return jax.nn.gelu(acc)
    if act == "relu2":
        r = jnp.maximum(acc, 0.0)
        return r * r
    return acc


def _mm_kernel(*refs, act, act_tiles, ck, nj, ni, side_blocks):
    ns = len(side_blocks)
    x_ref, wc_ref, b_ref, xs_ref = refs[:4]
    side_in = refs[4:4 + 2 * ns]
    o_ref, os_ref = refs[4 + 2 * ns:6 + 2 * ns]
    side_out = refs[6 + 2 * ns:6 + 3 * ns]
    wbf_ref = refs[6 + 3 * ns]
    j = pl.program_id(0)
    i = pl.program_id(1)

    for c, (tl, n_side) in enumerate(side_blocks):
        @pl.when(j * ni + i < n_side)
        def _(c=c, tl=tl):
            cur_ref, nxt_ref = side_in[2 * c:2 * c + 2]
            side_out[c][pl.ds(0, tl - 1)] = cur_ref[pl.ds(1, tl - 1)]
            side_out[c][pl.ds(tl - 1, 1)] = nxt_ref[...]

    @pl.when(j < nj)
    def _():
        r0 = pl.multiple_of(i * ck, ck)
        wbf_ref[j % 2, pl.ds(r0, ck), :] = wc_ref[...].astype(_BF16)

    def emit(lhs_ref, out_ref):
        acc = jnp.dot(lhs_ref[...], wbf_ref[(j - 1) % 2], preferred_element_type=_F32) + b_ref[...]
        if act_tiles is None:
            out_ref[...] = _apply_act(acc, act).astype(out_ref.dtype)
        else:
            @pl.when(j - 1 < act_tiles)
            def _():
                out_ref[...] = _apply_act(acc, act).astype(out_ref.dtype)

            @pl.when(j - 1 >= act_tiles)
            def _():
                out_ref[...] = acc.astype(out_ref.dtype)

    @pl.when(j > 0)
    def _():
        emit(x_ref, o_ref)

    @pl.when(jnp.logical_and(j > 0, i == 0))
    def _():
        emit(xs_ref, os_ref)


def _mm_acc_kernel(x_ref, w_ref, b_ref, xs_ref, o_ref, os_ref, *, n_sub):
    i = pl.program_id(1)
    k = pl.program_id(2)

    @pl.when(k == 0)
    def _():
        o_ref[...] = jnp.broadcast_to(b_ref[...], o_ref.shape)

    wb = w_ref[...].astype(_BF16)
    sub = wb.shape[0] // n_sub

    def contract(lhs_ref):
        return sum(jnp.dot(lhs_ref[s], wb[s * sub:(s + 1) * sub], preferred_element_type=_F32)
                   for s in range(n_sub))

    o_ref[...] += contract(x_ref)

    @pl.when(jnp.logical_and(i == 0, k == 0))
    def _():
        os_ref[...] = jnp.broadcast_to(b_ref[...], os_ref.shape)

    @pl.when(i == 0)
    def _():
        os_ref[...] += contract(xs_ref)


def _roll_rows(cache, steps):
    bsz, length = cache.shape[:2]
    tl = min(SIDE_ROLL_MIN_ROWS, length)
    while bsz * (length // tl) > steps:
        assert tl < length
        tl *= 2
    return tl


def _side_roll_specs(cache, tl, ni):
    bsz, length = cache.shape[:2]
    tail = cache.shape[2:]
    assert length % tl == 0
    per_batch = length // tl
    n_side = bsz * per_batch
    zeros = (0,) * len(tail)

    def where(j, i):
        t = jnp.minimum(j * ni + i, n_side - 1)
        return t // per_batch, t % per_batch

    def cur_map(j, i):
        return where(j, i) + zeros

    def nxt_map(j, i):
        bi, ti = where(j, i)
        return (bi, jnp.minimum((ti + 1) * tl, length - 1)) + zeros

    cur = pl.BlockSpec((None, tl) + tail, cur_map)
    nxt = pl.BlockSpec((None, 1) + tail, nxt_map)
    return cur, nxt, n_side


def matmul(x, xs, w, layer, b, *, act=None, act_cols=None, out_dtype=_F32, side_rolls=(), tiled_out=False):
    m, kdim = x.shape
    ms = xs.shape[0]
    n = w.shape[2]
    tm = min(MM_TILE, m)
    tn = min(MM_TILE, n)
    assert kdim <= MM_FULL_K and m % tm == 0 and n % tn == 0
    ni, nj = m // tm, n // tn
    assert kdim % ni == 0
    ck = kdim // ni
    act_tiles = None
    if act_cols is not None:
        assert act_cols % tn == 0
        act_tiles = act_cols // tn
    side_in_specs, side_out_specs, side_blocks, side_args = [], [], [], []
    for cache in side_rolls:
        tl = _roll_rows(cache, (nj + 1) * ni)
        cur, nxt, n_side = _side_roll_specs(cache, tl, ni)
        side_in_specs += [cur, nxt]
        side_out_specs.append(cur)
        side_blocks.append((tl, n_side))
        side_args += [cache, cache]
    row_tile = lambda j, i: jnp.where(j == 0, 0, i)
    col_tile = lambda j, i: jnp.maximum(j - 1, 0)
    if tiled_out:
        shapes = [(nj, m, tn), (nj, ms, tn)]
        o_spec = pl.BlockSpec((None, tm, tn), lambda j, i: (col_tile(j, i), row_tile(j, i), 0))
        os_spec = pl.BlockSpec((None, ms, tn), lambda j, i: (col_tile(j, i), 0, 0))
    else:
        shapes = [(m, n), (ms, n)]
        o_spec = pl.BlockSpec((tm, tn), lambda j, i: (row_tile(j, i), col_tile(j, i)))
        os_spec = pl.BlockSpec((ms, tn), lambda j, i: (0, col_tile(j, i)))
    return pl.pallas_call(
        functools.partial(_mm_kernel, act=act, act_tiles=act_tiles, ck=ck, nj=nj, ni=ni,
                          side_blocks=tuple(side_blocks)),
        out_shape=(*[jax.ShapeDtypeStruct(s, out_dtype) for s in shapes],
                   *[jax.ShapeDtypeStruct(c.shape, c.dtype) for c in side_rolls]),
        grid=(nj + 1, ni),
        in_specs=[
            pl.BlockSpec((tm, kdim), lambda j, i: (row_tile(j, i), 0)),
            pl.BlockSpec((None, ck, tn),
                         lambda j, i: (layer, jnp.where(j < nj, i, ni - 1), jnp.minimum(j, nj - 1))),
            pl.BlockSpec((1, tn), lambda j, i: (0, col_tile(j, i))),
            pl.BlockSpec((ms, kdim), lambda j, i: (0, 0)),
            *side_in_specs,
        ],
        out_specs=(o_spec, os_spec, *side_out_specs),
        scratch_shapes=[pltpu.VMEM((2, kdim, tn), _BF16)],
        compiler_params=_params("arbitrary", "arbitrary"),
        name="matmul",
    )(x, w, b, xs, *side_args)


def matmul_ktiled(x, xs, w, layer, b):
    nkt, m, tk = x.shape
    ms = xs.shape[1]
    n = w.shape[2]
    tm = min(MM_KACC_TILE_M, m)
    tn = min(MM_TILE, n)
    n_sub = min(MM_KACC_SUBTILES, nkt)
    assert m % tm == 0 and n % tn == 0 and nkt % n_sub == 0 and w.shape[1] == nkt * tk
    return pl.pallas_call(
        functools.partial(_mm_acc_kernel, n_sub=n_sub),
        out_shape=(jax.ShapeDtypeStruct((m, n), _F32), jax.ShapeDtypeStruct((ms, n), _F32)),
        grid=(n // tn, m // tm, nkt // n_sub),
        in_specs=[
            pl.BlockSpec((n_sub, tm, tk), lambda j, i, k: (k, i, 0)),
            pl.BlockSpec((None, n_sub * tk, tn), lambda j, i, k: (layer, k, j)),
            pl.BlockSpec((1, tn), lambda j, i, k: (0, j)),
            pl.BlockSpec((n_sub, ms, tk), lambda j, i, k: (k, 0, 0)),
        ],
        out_specs=(pl.BlockSpec((tm, tn), lambda j, i, k: (i, j)),
                   pl.BlockSpec((ms, tn), lambda j, i, k: (0, j))),
        compiler_params=_params("arbitrary", "arbitrary", "arbitrary"),
        name="matmul_ktiled",
    )(x, w, b, xs)


def _ln_kernel(x_ref, y_ref, g_ref, b_ref, o_ref, obf_ref):
    z = ALPHA * x_ref[...] + y_ref[...]
    mu = jnp.mean(z, axis=-1, keepdims=True)
    zc = z - mu
    var = jnp.mean(zc * zc, axis=-1, keepdims=True)
    out = zc * lax.rsqrt(var + LN_EPS) * g_ref[...] + b_ref[...]
    o_ref[...] = out
    obf_ref[...] = out.astype(_BF16)


def deepnorm(x, y, g, b):
    m, d = x.shape
    tm = min(LN_ROWS, m)
    assert m % tm == 0
    row = pl.BlockSpec((tm, d), lambda i: (i, 0))
    vec = pl.BlockSpec((1, d), lambda i: (0, 0))
    return pl.pallas_call(
        _ln_kernel,
        out_shape=(jax.ShapeDtypeStruct((m, d), _F32), jax.ShapeDtypeStruct((m, d), _BF16)),
        grid=(m // tm,),
        in_specs=[row, row, vec, vec],
        out_specs=(row, row),
        compiler_params=_params("arbitrary"),
        name="deepnorm",
    )(x, y, g, b)


def _softplus(x):
    return jnp.maximum(x, 0.0) + jnp.log1p(jnp.exp(-jnp.abs(x)))


def _lru_coeffs(xc, wra, bra, wix, bix, sp):
    xcb = xc.astype(_BF16)
    r = jax.nn.sigmoid(jnp.dot(xcb, wra, preferred_element_type=_F32) + bra)
    i = jax.nn.sigmoid(jnp.dot(xcb, wix, preferred_element_type=_F32) + bix)
    log_a = -LRU_C * r * sp
    a = jnp.exp(log_a)
    t = jnp.tanh(log_a)
    bt = jnp.sqrt(-2.0 * t / (1.0 - t)) * (i * xc)
    return a, bt


def _shift_rows(x, s, fill):
    rows = lax.broadcasted_iota(jnp.int32, x.shape, 0)
    return jnp.where(rows < s, fill, pltpu.roll(x, s, 0))


def _rglru_kernel(xb_ref, gate_ref, cw_ref, cb_ref, wra_ref, bra_ref, wix_ref, bix_ref, lam_ref,
                  o_ref, hlast_ref, xpad_ref, *, seq, chunk):
    pad = V7X_SUBLANES
    cw = cw_ref[...]
    cbias = cb_ref[...]
    wra = wra_ref[...].astype(_BF16)
    wix = wix_ref[...].astype(_BF16)
    bra = bra_ref[...]
    bix = bix_ref[...]
    sp = _softplus(-lam_ref[...])

    xpad_ref[pl.ds(0, pad), :] = jnp.zeros((pad, xpad_ref.shape[1]), _F32)
    xpad_ref[pl.ds(pad, seq), :] = xb_ref[...]

    def body(c, h):
        t0 = pl.multiple_of(c * chunk, chunk)
        xw = xpad_ref[pl.ds(t0, chunk + pad), :]
        xc = cbias
        for k in range(CONV_W):
            off = pad - (CONV_W - 1) + k
            xc = xc + cw[k:k + 1, :] * xw[off:off + chunk, :]
        a, bt = _lru_coeffs(xc, wra, bra, wix, bix, sp)
        hs = []
        for q in range(chunk // SCAN_SUB):
            aq = a[q * SCAN_SUB:(q + 1) * SCAN_SUB]
            bq = bt[q * SCAN_SUB:(q + 1) * SCAN_SUB]
            s = 1
            while s < SCAN_SUB:
                if s < V7X_SUBLANES:
                    a_sh = _shift_rows(aq, s, 1.0)
                    b_sh = _shift_rows(bq, s, 0.0)
                    bq = aq * b_sh + bq
                    aq = aq * a_sh
                else:
                    bq = jnp.concatenate([bq[:s], aq[s:] * bq[:-s] + bq[s:]], axis=0)
                    aq = jnp.concatenate([aq[:s], aq[s:] * aq[:-s]], axis=0)
                s *= 2
            hq = aq * h + bq
            h = hq[SCAN_SUB - 1:SCAN_SUB, :]
            hs.append(hq)
        hs = jnp.concatenate(hs, axis=0)
        o_ref[pl.ds(t0, chunk), :] = (hs * gate_ref[pl.ds(t0, chunk), :]).astype(o_ref.dtype)
        return h

    h0 = jnp.zeros((1, xb_ref.shape[1]), _F32)
    hlast_ref[...] = lax.fori_loop(0, seq // chunk, body, h0)


def rglru_prompt(u, cw, cb, wra, bra, wix, bix, lam):
    bsz, seq, c2 = u.shape
    c = c2 // 2
    nblk, blk, _ = wra.shape
    assert c == nblk * blk and seq % SCAN_CHUNK == 0
    col = lambda b, n: (b, 0, n)
    vec = pl.BlockSpec((1, blk), lambda b, n: (0, n))
    wspec = pl.BlockSpec((None, blk, blk), lambda b, n: (n, 0, 0))
    return pl.pallas_call(
        functools.partial(_rglru_kernel, seq=seq, chunk=SCAN_CHUNK),
        out_shape=(jax.ShapeDtypeStruct((bsz, seq, c), _BF16), jax.ShapeDtypeStruct((bsz, 1, c), _F32)),
        grid=(bsz, nblk),
        in_specs=[
            pl.BlockSpec((None, seq, blk), lambda b, n: (b, 0, n + nblk)),
            pl.BlockSpec((None, seq, blk), col),
            pl.BlockSpec((CONV_W, blk), lambda b, n: (0, n)),
            vec, wspec, vec, wspec, vec, vec,
        ],
        out_specs=(pl.BlockSpec((None, seq, blk), col), pl.BlockSpec((None, 1, blk), col)),
        scratch_shapes=[pltpu.VMEM((seq + V7X_SUBLANES, blk), _F32)],
        compiler_params=_params("arbitrary", "arbitrary"),
        name="rglru_prompt",
    )(u, u, cw, cb, wra, bra, wix, bix, lam)


def _rglru_step_kernel(xb_ref, gate_ref, c0_ref, c1_ref, c2_ref, h_ref, cw_ref, cb_ref,
                       wra_ref, bra_ref, wix_ref, bix_ref, lam_ref, o_ref, hnew_ref):
    cw = cw_ref[...]
    xc = (cb_ref[...] + cw[0:1, :] * c0_ref[...] + cw[1:2, :] * c1_ref[...]
          + cw[2:3, :] * c2_ref[...] + cw[3:4, :] * xb_ref[...])
    a, bt = _lru_coeffs(xc, wra_ref[...].astype(_BF16), bra_ref[...], wix_ref[...].astype(_BF16),
                        bix_ref[...], _softplus(-lam_ref[...]))
    h = a * h_ref[...] + bt
    hnew_ref[...] = h
    o_ref[...] = (h * gate_ref[...]).astype(o_ref.dtype)


def rglru_step(u, conv_prev, h_prev, cw, cb, wra, bra, wix, bix, lam):
    rows, c2 = u.shape
    c = c2 // 2
    nblk, blk, _ = wra.shape
    col = pl.BlockSpec((rows, blk), lambda n: (0, n))
    vec = pl.BlockSpec((1, blk), lambda n: (0, n))
    wspec = pl.BlockSpec((None, blk, blk), lambda n: (n, 0, 0))
    return pl.pallas_call(
        _rglru_step_kernel,
        out_shape=(jax.ShapeDtypeStruct((rows, c), _BF16), jax.ShapeDtypeStruct((rows, c), _F32)),
        grid=(nblk,),
        in_specs=[pl.BlockSpec((rows, blk), lambda n: (0, n + nblk)), col, col, col, col, col,
                  pl.BlockSpec((CONV_W, blk), lambda n: (0, n)), vec, wspec, vec, wspec, vec, vec],
        out_specs=(col, col),
        compiler_params=_params("arbitrary"),
        name="rglru_step",
    )(u, u, *conv_prev, h_prev, cw, cb, wra, bra, wix, bix, lam)


def _attn_prompt_kernel(*refs, seq):
    qkv_refs = refs[:9]
    o_ref = refs[9]
    acc_ref, m_ref, l_ref, stage_in_ref, stage_out_ref = refs[10:15]
    kb = KEYS_PER_BLOCK
    scale = HEAD_DIM ** -0.5
    qi = lax.broadcasted_iota(jnp.int32, (kb, kb), 0)
    ki = lax.broadcasted_iota(jnp.int32, (kb, kb), 1)
    cur_mask = ki <= qi
    qi2 = lax.broadcasted_iota(jnp.int32, (kb, 2 * kb), 0)
    ki2 = lax.broadcasted_iota(jnp.int32, (kb, 2 * kb), 1)
    cur_part = jnp.logical_and(ki2 >= kb, ki2 - kb <= qi2)
    prev_part = jnp.logical_and(ki2 < kb, ki2 >= qi2)
    n_blocks = seq // kb
    assert n_blocks % ATTN_UNROLL == 0

    def strided(start, n, st):
        return pl.ds(start, n) if st == 1 else pl.ds(start, n, stride=st)

    def run_group(load, store, st, seg_len):
        per_res = seg_len // st
        nblk = per_res // kb
        assert per_res % kb == 0 and n_blocks % (st * nblk) == 0

        def block_rows(idx):
            seg = idx // (st * nblk)
            r = (idx // nblk) % st
            j = idx % nblk
            first = seg * seg_len + r
            start = first + j * (kb * st)
            prows = strided(jnp.maximum(start - kb * st, first), kb, st)
            return strided(start, kb, st), prows, j

        def scores(rows, prows, j):
            q = load(0, rows).astype(_BF16)
            k = load(1, rows).astype(_BF16)
            if nblk > 1:
                k = jnp.concatenate([load(1, prows).astype(_BF16), k], axis=0)
                mask = jnp.logical_or(cur_part, jnp.logical_and(prev_part, j > 0))
            else:
                mask = cur_mask
            s = lax.dot_general(q, k, (((1,), (1,)), ((), ())), preferred_element_type=_F32) * scale
            return jnp.where(mask, s, -jnp.inf)

        def values(rows, prows):
            v = load(2, rows).astype(_BF16)
            if nblk > 1:
                v = jnp.concatenate([load(2, prows).astype(_BF16), v], axis=0)
            return v

        def blocks(it, carry):
            where = [block_rows(it * ATTN_UNROLL + u) for u in range(ATTN_UNROLL)]
            ss = [scores(rows, prows, j) for rows, prows, j in where]
            ms = [jnp.max(s, axis=-1, keepdims=True) for s in ss]
            ps = [jnp.exp(s - m) for s, m in zip(ss, ms)]
            for (rows, prows, _), m, p in zip(where, ms, ps):
                store(0, rows, jnp.dot(p.astype(_BF16), values(rows, prows), preferred_element_type=_F32))
                store(1, rows, jnp.broadcast_to(m, (kb, HEAD_DIM)))
                store(2, rows, jnp.broadcast_to(jnp.sum(p, axis=-1, keepdims=True), (kb, HEAD_DIM)))
            return carry

        lax.fori_loop(0, n_blocks // ATTN_UNROLL, blocks, 0)

    stats = (acc_ref, m_ref, l_ref)
    for g, (_, dil) in enumerate(GROUPS):
        src = qkv_refs[3 * g:3 * g + 3]

        def store_tok(a, rows, val, g=g):
            stats[a][g, rows, :] = val

        if dil <= ATTN_MAX_STRIDE:
            run_group(lambda a, rows, src=src: src[a][rows, :], store_tok, dil, seq)
            continue
        inner = ATTN_MAX_STRIDE
        outer = dil // inner
        assert dil == inner * outer
        seg_len = seq // outer
        for a in range(3):
            for r1 in range(outer):
                for c in range(seg_len // kb):
                    stage_in_ref[a, pl.ds(r1 * seg_len + c * kb, kb), :] = (
                        src[a][strided(r1 + c * kb * outer, kb, outer), :])

        def store_seg(a, rows, val):
            stage_out_ref[a, rows, :] = val

        run_group(lambda a, rows: stage_in_ref[a, rows, :], store_seg, inner, seg_len)
        for a in range(3):
            for r1 in range(outer):
                for c in range(seg_len // kb):
                    stats[a][g, strided(r1 + c * kb * outer, kb, outer), :] = (
                        stage_out_ref[a, pl.ds(r1 * seg_len + c * kb, kb), :])

    def merge(c, carry):
        t0 = pl.multiple_of(c * kb, kb)
        ms = [m_ref[g, pl.ds(t0, kb), :] for g in range(len(GROUPS))]
        mx = functools.reduce(jnp.maximum, ms)
        num = jnp.zeros((kb, HEAD_DIM), _F32)
        den = jnp.zeros((kb, HEAD_DIM), _F32)
        for g in range(len(GROUPS)):
            w = jnp.exp(ms[g] - mx)
            num = num + w * acc_ref[g, pl.ds(t0, kb), :]
            den = den + w * l_ref[g, pl.ds(t0, kb), :]
        o_ref[pl.ds(t0, kb), :] = (num / den).astype(o_ref.dtype)
        return carry

    lax.fori_loop(0, seq // kb, merge, 0)


def attn_prompt(qkv, n_heads):
    bsz, seq, _ = qkv.shape
    ng = len(GROUPS)

    def spec(section):
        return pl.BlockSpec((None, seq, HEAD_DIM), lambda b, h: (b, 0, section * n_heads + h))

    return pl.pallas_call(
        functools.partial(_attn_prompt_kernel, seq=seq),
        out_shape=jax.ShapeDtypeStruct((bsz, seq, n_heads * HEAD_DIM), _BF16),
        grid=(bsz, n_heads),
        in_specs=[spec(s) for s in range(3 * ng)],
        out_specs=pl.BlockSpec((None, seq, HEAD_DIM), lambda b, h: (b, 0, h)),
        scratch_shapes=[pltpu.VMEM((ng, seq, HEAD_DIM), _F32)] * 3 + [pltpu.VMEM((3, seq, HEAD_DIM), _F32)] * 2,
        compiler_params=_params("arbitrary", "arbitrary"),
        name="attn_prompt",
    )(*([qkv] * (3 * ng)))


def _attn_sample_kernel(*refs):
    ng = len(GROUPS)
    q_refs = refs[0:ng]
    knew_refs = refs[ng:2 * ng]
    vnew_refs = refs[2 * ng:3 * ng]
    kc_refs = refs[3 * ng:4 * ng]
    vc_refs = refs[4 * ng:5 * ng]
    o_ref = refs[5 * ng]
    scale = HEAD_DIM ** -0.5
    parts = []
    for g in range(ng):
        q = q_refs[g][...]
        s_new = jnp.sum(q * knew_refs[g][...], axis=-1, keepdims=True) * scale
        s_old = jnp.sum(q[None] * kc_refs[g][...], axis=-1, keepdims=True) * scale
        m = jnp.maximum(jnp.max(s_old, axis=0), s_new)
        p_new = jnp.exp(s_new - m)
        p_old = jnp.exp(s_old - m[None])
        l = p_new + jnp.sum(p_old, axis=0)
        acc = p_new * vnew_refs[g][...] + jnp.sum(p_old * vc_refs[g][...], axis=0)
        parts.append((m, l, acc))
    mx = functools.reduce(jnp.maximum, [p[0] for p in parts])
    num = sum(jnp.exp(m - mx) * acc for m, _, acc in parts)
    den = sum(jnp.exp(m - mx) * l for m, l, _ in parts)
    o_ref[...] = (num / den).astype(o_ref.dtype)


def attn_sample(qkv, caches, n_heads):
    bsz = qkv.shape[0]
    ng = len(GROUPS)
    kb = KEYS_PER_BLOCK

    def new_spec(section):
        return pl.BlockSpec((None, None, n_heads, HEAD_DIM), lambda b: (b, section, 0, 0))

    def cache_spec(kv):
        return pl.BlockSpec((None, kb, None, None, n_heads, HEAD_DIM), lambda b: (b, 0, 0, kv, 0, 0))

    strided = []
    for (win, dil), cache in zip(GROUPS, caches):
        assert cache.shape[1] == win == kb * dil
        strided.append(cache.reshape(bsz, kb, dil, 2, n_heads, HEAD_DIM))
    in_specs = ([new_spec(3 * g) for g in range(ng)] + [new_spec(3 * g + 1) for g in range(ng)]
                + [new_spec(3 * g + 2) for g in range(ng)] + [cache_spec(0)] * ng + [cache_spec(1)] * ng)
    return pl.pallas_call(
        _attn_sample_kernel,
        out_shape=jax.ShapeDtypeStruct((bsz, n_heads, HEAD_DIM), _BF16),
        grid=(bsz,),
        in_specs=in_specs,
        out_specs=pl.BlockSpec((None, n_heads, HEAD_DIM), lambda b: (b, 0, 0)),
        compiler_params=_params("arbitrary"),
        name="attn_sample",
    )(*([qkv] * (3 * ng)), *strided, *strided)


def _set_last_kernel(rolled_ref, new_ref, o_ref):
    del rolled_ref
    o_ref[...] = new_ref[...]


def cache_set_last(rolled, new_row):
    bsz, length = rolled.shape[:2]
    tail = rolled.shape[2:]
    zeros = (0,) * len(tail)
    return pl.pallas_call(
        _set_last_kernel,
        out_shape=jax.ShapeDtypeStruct(rolled.shape, rolled.dtype),
        grid=(bsz,),
        in_specs=[pl.BlockSpec(memory_space=pl.ANY),
                  pl.BlockSpec((None, 1) + tail, lambda b: (b, 0) + zeros)],
        out_specs=pl.BlockSpec((None, 1) + tail, lambda b: (b, length - 1) + zeros),
        input_output_aliases={0: 0},
        compiler_params=_params("arbitrary"),
        name="cache_set_last",
    )(rolled, new_row)


def _pad_rows(a, rows):
    return jnp.pad(a, ((0, rows - a.shape[0]),) + ((0, 0),) * (a.ndim - 1))


def kernel(x_prompt, x_sample, state_conv, state_h, cache_kv_w128, cache_kv_w512, cache_kv_w2048,
           lru_w_in, lru_b_in, lru_conv_w, lru_conv_b, lru_w_ra, lru_b_ra, lru_w_ix, lru_b_ix,
           lru_lambda, lru_w_out, lru_b_out, attn_w_qkv, attn_w_o,
           mlp_w1, mlp_b1, mlp_w2, mlp_b2, ln_g, ln_b):
    bp, seq, d = x_prompt.shape
    bs, dec_seq, _ = x_sample.shape
    assert dec_seq == 1
    n_heads = d // HEAD_DIM
    ng = len(GROUPS)
    mp = bp * seq
    ms = -(-bs // V7X_BF16_SUBLANES) * V7X_BF16_SUBLANES
    caches = (cache_kv_w128[0], cache_kv_w512[0], cache_kv_w2048[0])
    row = lambda v: v[:, None, :]
    b_in, conv_b, b_ra, b_ix, lam, b_out = (row(v) for v in (lru_b_in, lru_conv_b, lru_b_ra, lru_b_ix,
                                                               lru_lambda, lru_b_out))
    b1, b2 = row(mlp_b1), row(mlp_b2)
    g_ln, b_ln = ln_g[:, :, None, :], ln_b[:, :, None, :]

    xp = x_prompt.reshape(mp, d)
    xs = _pad_rows(x_sample.reshape(bs, d), ms)

    def norm(xf, y, i, k):
        return deepnorm(xf, y, g_ln[i][k], b_ln[i][k])

    def mlp(xfp, xbp, xfs, xbs, i, side_rolls=()):
        hp, hs, *rolled = matmul(xbp, xbs, mlp_w1, i, b1[i], act="relu2", out_dtype=_BF16,
                                 tiled_out=True, side_rolls=side_rolls)
        yp, ys = matmul_ktiled(hp, hs, mlp_w2, i, b2[i])
        return norm(xfp, yp, i, 1), norm(xfs, ys, i, 1), rolled

    d_rnn = lru_conv_w.shape[2]
    up, us = matmul(xp.astype(_BF16), xs.astype(_BF16), lru_w_in, 0, b_in[0],
                    act="gelu", act_cols=d_rnn)
    lru_args = (lru_conv_w[0], conv_b[0], lru_w_ra[0], b_ra[0], lru_w_ix[0], b_ix[0], lam[0])
    up3 = up.reshape(bp, seq, 2 * d_rnn)
    hgp, h_p = rglru_prompt(up3, *lru_args)
    conv_p = up3[:, seq - (CONV_W - 1):, d_rnn:]
    conv_prev = [_pad_rows(state_conv[0][:, k], ms) for k in range(CONV_W - 1)]
    hgs, h_s = rglru_step(us, conv_prev, _pad_rows(state_h[0], ms), *lru_args)
    conv_s = jnp.concatenate([state_conv[0][:, 1:], us[:bs, None, d_rnn:]], axis=1)
    yp, ys = matmul(hgp.reshape(mp, d_rnn), hgs, lru_w_out, 0, b_out[0])
    (xfp, xbp), (xfs, xbs) = norm(xp, yp, 0, 0), norm(xs, ys, 0, 0)
    (xfp, xbp), (xfs, xbs), _ = mlp(xfp, xbp, xfs, xbs, 0)

    zero_bias = jnp.zeros((1, attn_w_qkv.shape[2]), _F32)
    qkv_p, qkv_s, rolled_big = matmul(xbp, xbs, attn_w_qkv, 0, zero_bias, side_rolls=caches[-1:])
    qkv_p = qkv_p.reshape(bp, seq, -1)
    op = attn_prompt(qkv_p, n_heads).reshape(mp, d)
    sec = n_heads * HEAD_DIM
    kv_p = []
    for g, (win, _) in enumerate(GROUPS):
        keep = min(win, seq)
        kv = qkv_p[:, seq - keep:, (3 * g + 1) * sec:(3 * g + 3) * sec]
        kv_p.append(kv.reshape(1, bp, keep, 2, n_heads, HEAD_DIM))
    qkv_s = qkv_s[:bs].reshape(bs, 3 * ng, n_heads, HEAD_DIM)
    os_ = _pad_rows(attn_sample(qkv_s, caches, n_heads).reshape(bs, d), ms)
    yp, ys = matmul(op, os_, attn_w_o, 0, jnp.zeros((1, d), _F32))
    (xfp, xbp), (xfs, xbs) = norm(xfp, yp, 1, 0), norm(xfs, ys, 1, 0)
    (xfp, _), (xfs, _), rolled_small = mlp(xfp, xbp, xfs, xbs, 1, side_rolls=caches[:-1])

    rolled = (*rolled_small, rolled_big)
    kv_s = []
    for g in range(ng):
        new_row = qkv_s[:, 3 * g + 1:3 * g + 3].reshape(bs, 1, 2, n_heads, HEAD_DIM)
        kv_s.append(cache_set_last(rolled[g], new_row)[None])

    y_p = xfp.reshape(bp, seq, d)
    y_s = xfs[:bs].reshape(bs, 1, d)
    return (y_p, y_s, conv_p[None], h_p.reshape(1, bp, d_rnn), kv_p[0], kv_p[1], kv_p[2],
            conv_s[None], h_s[:bs][None], kv_s[0], kv_s[1], kv_s[2])
```

```python
import functools

import jax
import jax.numpy as jnp
from jax import lax
from jax.experimental import pallas as pl
from jax.experimental.pallas import tpu as pltpu

HEAD_DIM = 128
CONV_W = 4
LRU_C = 8.0
GROUPS = ((128, 1), (512, 4), (2048, 16))
KEYS_PER_BLOCK = 128
LN_EPS = 1e-5
DEPTH = 2
ALPHA = (2.0 * DEPTH) ** 0.25

V7X_SUBLANES = 8
V7X_BF16_SUBLANES = 16
V7X_VMEM_LIMIT_BYTES = 62 * 1024 * 1024

MM_TILE = 1024
MM_FULL_K = 4096
MM_KACC_TILE_M = 2048
MM_KACC_SUBTILES = 1
ATTN_UNROLL = 8
ATTN_MAX_STRIDE = 4
SIDE_ROLL_MIN_ROWS = 32
LN_ROWS = 256
SCAN_CHUNK = 256
SCAN_SUB = 64

_BF16 = jnp.bfloat16
_F32 = jnp.float32


def _params(*sem):
    return pltpu.CompilerParams(dimension_semantics=sem, vmem_limit_bytes=V7X_VMEM_LIMIT_BYTES)


def _apply_act(acc, act):
    if act == "gelu":
        return jax.nn.gelu(acc)
    if act == "relu2":
        r = jnp.maximum(acc, 0.0)
        return r * r
    return acc


def _roll_block(cur_ref, nxt_ref, out_ref):
    tl = out_ref.shape[0]
    out_ref[pl.ds(0, tl - 1)] = cur_ref[pl.ds(1, tl - 1)]
    out_ref[pl.ds(tl - 1, 1)] = nxt_ref[...]


def _mm_kernel(*refs, act, act_tiles, ck, nj, ni, side_blocks, slabs):
    ns = len(side_blocks)
    x_ref, wc_ref, b_ref, xs_ref = refs[:4]
    side_in = refs[4:4 + 2 * ns]
    o_ref, os_ref = refs[4 + 2 * ns:6 + 2 * ns]
    side_out = refs[6 + 2 * ns:6 + 3 * ns]
    wbf_ref = refs[6 + 3 * ns]
    j = pl.program_id(0)
    i = pl.program_id(1)

    for c, n_side in enumerate(side_blocks):
        @pl.when(j * ni + i < n_side)
        def _(c=c):
            _roll_block(side_in[2 * c], side_in[2 * c + 1], side_out[c])

    @pl.when(j < nj)
    def _():
        r0 = pl.multiple_of(i * ck, ck)
        wbf_ref[j % 2, pl.ds(r0, ck), :] = wc_ref[...].astype(_BF16)

    def put(out_ref, val):
        if not slabs:
            out_ref[...] = val.astype(out_ref.dtype)
            return
        for h in range(out_ref.shape[0]):
            out_ref[h] = val[:, h * HEAD_DIM:(h + 1) * HEAD_DIM].astype(out_ref.dtype)

    def emit(lhs_ref, out_ref):
        acc = jnp.dot(lhs_ref[...], wbf_ref[(j - 1) % 2], preferred_element_type=_F32) + b_ref[...]
        if act_tiles is None:
            put(out_ref, _apply_act(acc, act))
        else:
            @pl.when(j - 1 < act_tiles)
            def _():
                put(out_ref, _apply_act(acc, act))

            @pl.when(j - 1 >= act_tiles)
            def _():
                put(out_ref, acc)

    @pl.when(j > 0)
    def _():
        emit(x_ref, o_ref)

    @pl.when(jnp.logical_and(j > 0, i == 0))
    def _():
        emit(xs_ref, os_ref)


def _mm_acc_kernel(x_ref, w_ref, b_ref, xs_ref, o_ref, os_ref, *, n_sub):
    i = pl.program_id(1)
    k = pl.program_id(2)

    @pl.when(k == 0)
    def _():
        o_ref[...] = jnp.broadcast_to(b_ref[...], o_ref.shape)

    wb = w_ref[...].astype(_BF16)
    sub = wb.shape[0] // n_sub

    def contract(lhs_ref):
        return sum(jnp.dot(lhs_ref[s], wb[s * sub:(s + 1) * sub], preferred_element_type=_F32)
                   for s in range(n_sub))

    o_ref[...] += contract(x_ref)

    @pl.when(jnp.logical_and(i == 0, k == 0))
    def _():
        os_ref[...] = jnp.broadcast_to(b_ref[...], os_ref.shape)

    @pl.when(i == 0)
    def _():
        os_ref[...] += contract(xs_ref)


def _roll_rows(cache, steps):
    bsz, length = cache.shape[:2]
    tl = min(SIDE_ROLL_MIN_ROWS, length)
    while bsz * (length // tl) > steps:
        assert tl < length
        tl *= 2
    return tl


def _side_roll_specs(cache, tl, ni):
    bsz, length = cache.shape[:2]
    tail = cache.shape[2:]
    assert length % tl == 0
    per_batch = length // tl
    n_side = bsz * per_batch
    zeros = (0,) * len(tail)

    def where(j, i):
        t = jnp.minimum(j * ni + i, n_side - 1)
        return t // per_batch, t % per_batch

    def cur_map(j, i):
        return where(j, i) + zeros

    def nxt_map(j, i):
        bi, ti = where(j, i)
        return (bi, jnp.minimum((ti + 1) * tl, length - 1)) + zeros

    cur = pl.BlockSpec((None, tl) + tail, cur_map)
    nxt = pl.BlockSpec((None, 1) + tail, nxt_map)
    return cur, nxt, n_side


def matmul(x, xs, w, layer, b, *, act=None, act_cols=None, out_dtype=_F32, side_rolls=(), tiled_out=False,
           slabs=False):
    m, kdim = x.shape
    ms = xs.shape[0]
    n = w.shape[2]
    tm = min(MM_TILE, m)
    tn = min(MM_TILE, n)
    assert kdim <= MM_FULL_K and m % tm == 0 and n % tn == 0
    ni, nj = m // tm, n // tn
    assert kdim % ni == 0
    ck = kdim // ni
    act_tiles = None
    if act_cols is not None:
        assert act_cols % tn == 0
        act_tiles = act_cols // tn
    side_in_specs, side_out_specs, side_blocks, side_args = [], [], [], []
    for cache in side_rolls:
        tl = _roll_rows(cache, (nj + 1) * ni)
        cur, nxt, n_side = _side_roll_specs(cache, tl, ni)
        side_in_specs += [cur, nxt]
        side_out_specs.append(cur)
        side_blocks.append(n_side)
        side_args += [cache, cache]
    row_tile = lambda j, i: jnp.where(j == 0, 0, i)
    col_tile = lambda j, i: jnp.maximum(j - 1, 0)
    assert not (tiled_out and slabs)
    if tiled_out:
        shapes = [(nj, m, tn), (nj, ms, tn)]
        o_spec = pl.BlockSpec((None, tm, tn), lambda j, i: (col_tile(j, i), row_tile(j, i), 0))
        os_spec = pl.BlockSpec((None, ms, tn), lambda j, i: (col_tile(j, i), 0, 0))
    elif slabs:
        per_tile = tn // HEAD_DIM
        shapes = [(n // HEAD_DIM, m, HEAD_DIM), (n // HEAD_DIM, ms, HEAD_DIM)]
        o_spec = pl.BlockSpec((per_tile, tm, HEAD_DIM), lambda j, i: (col_tile(j, i), row_tile(j, i), 0))
        os_spec = pl.BlockSpec((per_tile, ms, HEAD_DIM), lambda j, i: (col_tile(j, i), 0, 0))
    else:
        shapes = [(m, n), (ms, n)]
        o_spec = pl.BlockSpec((tm, tn), lambda j, i: (row_tile(j, i), col_tile(j, i)))
        os_spec = pl.BlockSpec((ms, tn), lambda j, i: (0, col_tile(j, i)))
    return pl.pallas_call(
        functools.partial(_mm_kernel, act=act, act_tiles=act_tiles, ck=ck, nj=nj, ni=ni,
                          side_blocks=tuple(side_blocks), slabs=slabs),
        out_shape=(*[jax.ShapeDtypeStruct(s, out_dtype) for s in shapes],
                   *[jax.ShapeDtypeStruct(c.shape, c.dtype) for c in side_rolls]),
        grid=(nj + 1, ni),
        in_specs=[
            pl.BlockSpec((tm, kdim), lambda j, i: (row_tile(j, i), 0)),
            pl.BlockSpec((None, ck, tn),
                         lambda j, i: (layer, jnp.where(j < nj, i, ni - 1), jnp.minimum(j, nj - 1))),
            pl.BlockSpec((1, tn), lambda j, i: (0, col_tile(j, i))),
            pl.BlockSpec((ms, kdim), lambda j, i: (0, 0)),
            *side_in_specs,
        ],
        out_specs=(o_spec, os_spec, *side_out_specs),
        scratch_shapes=[pltpu.VMEM((2, kdim, tn), _BF16)],
        compiler_params=_params("arbitrary", "arbitrary"),
        name="matmul",
    )(x, w, b, xs, *side_args)


def matmul_ktiled(x, xs, w, layer, b):
    nkt, m, tk = x.shape
    ms = xs.shape[1]
    n = w.shape[2]
    tm = min(MM_KACC_TILE_M, m)
    tn = min(MM_TILE, n)
    n_sub = min(MM_KACC_SUBTILES, nkt)
    assert m % tm == 0 and n % tn == 0 and nkt % n_sub == 0 and w.shape[1] == nkt * tk
    return pl.pallas_call(
        functools.partial(_mm_acc_kernel, n_sub=n_sub),
        out_shape=(jax.ShapeDtypeStruct((m, n), _F32), jax.ShapeDtypeStruct((ms, n), _F32)),
        grid=(n // tn, m // tm, nkt // n_sub),
        in_specs=[
            pl.BlockSpec((n_sub, tm, tk), lambda j, i, k: (k, i, 0)),
            pl.BlockSpec((None, n_sub * tk, tn), lambda j, i, k: (layer, k, j)),
            pl.BlockSpec((1, tn), lambda j, i, k: (0, j)),
            pl.BlockSpec((n_sub, ms, tk), lambda j, i, k: (k, 0, 0)),
        ],
        out_specs=(pl.BlockSpec((tm, tn), lambda j, i, k: (i, j)),
                   pl.BlockSpec((ms, tn), lambda j, i, k: (0, j))),
        compiler_params=_params("arbitrary", "arbitrary", "arbitrary"),
        name="matmul_ktiled",
    )(x, w, b, xs)


def _ln_kernel(x_ref, y_ref, g_ref, b_ref, o_ref, obf_ref):
    z = ALPHA * x_ref[...] + y_ref[...]
    mu = jnp.mean(z, axis=-1, keepdims=True)
    zc = z - mu
    var = jnp.mean(zc * zc, axis=-1, keepdims=True)
    out = zc * lax.rsqrt(var + LN_EPS) * g_ref[...] + b_ref[...]
    o_ref[...] = out
    obf_ref[...] = out.astype(_BF16)


def deepnorm(x, y, g, b):
    m, d = x.shape
    tm = min(LN_ROWS, m)
    assert m % tm == 0
    row = pl.BlockSpec((tm, d), lambda i: (i, 0))
    vec = pl.BlockSpec((1, d), lambda i: (0, 0))
    return pl.pallas_call(
        _ln_kernel,
        out_shape=(jax.ShapeDtypeStruct((m, d), _F32), jax.ShapeDtypeStruct((m, d), _BF16)),
        grid=(m // tm,),
        in_specs=[row, row, vec, vec],
        out_specs=(row, row),
        compiler_params=_params("arbitrary"),
        name="deepnorm",
    )(x, y, g, b)


def _softplus(x):
    return jnp.maximum(x, 0.0) + jnp.log1p(jnp.exp(-jnp.abs(x)))


def _lru_coeffs(xc, wra, bra, wix, bix, sp):
    xcb = xc.astype(_BF16)
    r = jax.nn.sigmoid(jnp.dot(xcb, wra, preferred_element_type=_F32) + bra)
    i = jax.nn.sigmoid(jnp.dot(xcb, wix, preferred_element_type=_F32) + bix)
    log_a = -LRU_C * r * sp
    a = jnp.exp(log_a)
    t = jnp.tanh(log_a)
    bt = jnp.sqrt(-2.0 * t / (1.0 - t)) * (i * xc)
    return a, bt


def _shift_rows(x, s, fill):
    rows = lax.broadcasted_iota(jnp.int32, x.shape, 0)
    return jnp.where(rows < s, fill, pltpu.roll(x, s, 0))


def _rglru_kernel(xb_ref, gate_ref, cw_ref, cb_ref, wra_ref, bra_ref, wix_ref, bix_ref, lam_ref,
                  o_ref, hlast_ref, xpad_ref, *, seq, chunk):
    pad = V7X_SUBLANES
    cw = cw_ref[...]
    cbias = cb_ref[...]
    wra = wra_ref[...].astype(_BF16)
    wix = wix_ref[...].astype(_BF16)
    bra = bra_ref[...]
    bix = bix_ref[...]
    sp = _softplus(-lam_ref[...])

    xpad_ref[pl.ds(0, pad), :] = jnp.zeros((pad, xpad_ref.shape[1]), _F32)
    xpad_ref[pl.ds(pad, seq), :] = xb_ref[...]

    def body(c, h):
        t0 = pl.multiple_of(c * chunk, chunk)
        xw = xpad_ref[pl.ds(t0, chunk + pad), :]
        xc = cbias
        for k in range(CONV_W):
            off = pad - (CONV_W - 1) + k
            xc = xc + cw[k:k + 1, :] * xw[off:off + chunk, :]
        a, bt = _lru_coeffs(xc, wra, bra, wix, bix, sp)
        hs = []
        for q in range(chunk // SCAN_SUB):
            aq = a[q * SCAN_SUB:(q + 1) * SCAN_SUB]
            bq = bt[q * SCAN_SUB:(q + 1) * SCAN_SUB]
            s = 1
            while s < SCAN_SUB:
                if s < V7X_SUBLANES:
                    a_sh = _shift_rows(aq, s, 1.0)
                    b_sh = _shift_rows(bq, s, 0.0)
                    bq = aq * b_sh + bq
                    aq = aq * a_sh
                else:
                    bq = jnp.concatenate([bq[:s], aq[s:] * bq[:-s] + bq[s:]], axis=0)
                    aq = jnp.concatenate([aq[:s], aq[s:] * aq[:-s]], axis=0)
                s *= 2
            hq = aq * h + bq
            h = hq[SCAN_SUB - 1:SCAN_SUB, :]
            hs.append(hq)
        hs = jnp.concatenate(hs, axis=0)
        o_ref[pl.ds(t0, chunk), :] = (hs * gate_ref[pl.ds(t0, chunk), :]).astype(o_ref.dtype)
        return h

    h0 = jnp.zeros((1, xb_ref.shape[1]), _F32)
    hlast_ref[...] = lax.fori_loop(0, seq // chunk, body, h0)


def rglru_prompt(u, cw, cb, wra, bra, wix, bix, lam):
    bsz, seq, c2 = u.shape
    c = c2 // 2
    nblk, blk, _ = wra.shape
    assert c == nblk * blk and seq % SCAN_CHUNK == 0
    col = lambda b, n: (b, 0, n)
    vec = pl.BlockSpec((1, blk), lambda b, n: (0, n))
    wspec = pl.BlockSpec((None, blk, blk), lambda b, n: (n, 0, 0))
    return pl.pallas_call(
        functools.partial(_rglru_kernel, seq=seq, chunk=SCAN_CHUNK),
        out_shape=(jax.ShapeDtypeStruct((bsz, seq, c), _BF16), jax.ShapeDtypeStruct((bsz, 1, c), _F32)),
        grid=(bsz, nblk),
        in_specs=[
            pl.BlockSpec((None, seq, blk), lambda b, n: (b, 0, n + nblk)),
            pl.BlockSpec((None, seq, blk), col),
            pl.BlockSpec((CONV_W, blk), lambda b, n: (0, n)),
            vec, wspec, vec, wspec, vec, vec,
        ],
        out_specs=(pl.BlockSpec((None, seq, blk), col), pl.BlockSpec((None, 1, blk), col)),
        scratch_shapes=[pltpu.VMEM((seq + V7X_SUBLANES, blk), _F32)],
        compiler_params=_params("arbitrary", "arbitrary"),
        name="rglru_prompt",
    )(u, u, cw, cb, wra, bra, wix, bix, lam)


def _rglru_step_kernel(xb_ref, gate_ref, c0_ref, c1_ref, c2_ref, h_ref, cw_ref, cb_ref,
                       wra_ref, bra_ref, wix_ref, bix_ref, lam_ref, o_ref, hnew_ref):
    cw = cw_ref[...]
    xc = (cb_ref[...] + cw[0:1, :] * c0_ref[...] + cw[1:2, :] * c1_ref[...]
          + cw[2:3, :] * c2_ref[...] + cw[3:4, :] * xb_ref[...])
    a, bt = _lru_coeffs(xc, wra_ref[...].astype(_BF16), bra_ref[...], wix_ref[...].astype(_BF16),
                        bix_ref[...], _softplus(-lam_ref[...]))
    h = a * h_ref[...] + bt
    hnew_ref[...] = h
    o_ref[...] = (h * gate_ref[...]).astype(o_ref.dtype)


def rglru_step(u, conv_prev, h_prev, cw, cb, wra, bra, wix, bix, lam):
    rows, c2 = u.shape
    c = c2 // 2
    nblk, blk, _ = wra.shape
    col = pl.BlockSpec((rows, blk), lambda n: (0, n))
    vec = pl.BlockSpec((1, blk), lambda n: (0, n))
    wspec = pl.BlockSpec((None, blk, blk), lambda n: (n, 0, 0))
    return pl.pallas_call(
        _rglru_step_kernel,
        out_shape=(jax.ShapeDtypeStruct((rows, c), _BF16), jax.ShapeDtypeStruct((rows, c), _F32)),
        grid=(nblk,),
        in_specs=[pl.BlockSpec((rows, blk), lambda n: (0, n + nblk)), col, col, col, col, col,
                  pl.BlockSpec((CONV_W, blk), lambda n: (0, n)), vec, wspec, vec, wspec, vec, vec],
        out_specs=(col, col),
        compiler_params=_params("arbitrary"),
        name="rglru_step",
    )(u, u, *conv_prev, h_prev, cw, cb, wra, bra, wix, bix, lam)


def _attn_prompt_kernel(*refs, seq, n_side, n_heads):
    qkv_refs = refs[:9]
    cur_ref, nxt_ref, o_ref, rolled_ref = refs[9:13]
    acc_ref, m_ref, l_ref, stage_in_ref, stage_out_ref = refs[13:18]

    @pl.when(pl.program_id(0) * n_heads + pl.program_id(1) < n_side)
    def _():
        _roll_block(cur_ref, nxt_ref, rolled_ref)

    kb = KEYS_PER_BLOCK
    scale = HEAD_DIM ** -0.5
    qi = lax.broadcasted_iota(jnp.int32, (kb, kb), 0)
    ki = lax.broadcasted_iota(jnp.int32, (kb, kb), 1)
    cur_mask = ki <= qi
    qi2 = lax.broadcasted_iota(jnp.int32, (kb, 2 * kb), 0)
    ki2 = lax.broadcasted_iota(jnp.int32, (kb, 2 * kb), 1)
    cur_part = jnp.logical_and(ki2 >= kb, ki2 - kb <= qi2)
    prev_part = jnp.logical_and(ki2 < kb, ki2 >= qi2)
    n_blocks = seq // kb
    assert n_blocks % ATTN_UNROLL == 0

    def strided(start, n, st):
        return pl.ds(start, n) if st == 1 else pl.ds(start, n, stride=st)

    def run_group(load, store, st, seg_len):
        per_res = seg_len // st
        nblk = per_res // kb
        assert per_res % kb == 0 and n_blocks % (st * nblk) == 0

        def block_rows(idx):
            seg = idx // (st * nblk)
            r = (idx // nblk) % st
            j = idx % nblk
            first = seg * seg_len + r
            start = first + j * (kb * st)
            prows = strided(jnp.maximum(start - kb * st, first), kb, st)
            return strided(start, kb, st), prows, j

        def scores(rows, prows, j):
            q = load(0, rows).astype(_BF16)
            k = load(1, rows).astype(_BF16)
            if nblk > 1:
                k = jnp.concatenate([load(1, prows).astype(_BF16), k], axis=0)
                mask = jnp.logical_or(cur_part, jnp.logical_and(prev_part, j > 0))
            else:
                mask = cur_mask
            s = lax.dot_general(q, k, (((1,), (1,)), ((), ())), preferred_element_type=_F32) * scale
            return jnp.where(mask, s, -jnp.inf)

        def values(rows, prows):
            v = load(2, rows).astype(_BF16)
            if nblk > 1:
                v = jnp.concatenate([load(2, prows).astype(_BF16), v], axis=0)
            return v

        def blocks(it, carry):
            where = [block_rows(it * ATTN_UNROLL + u) for u in range(ATTN_UNROLL)]
            ss = [scores(rows, prows, j) for rows, prows, j in where]
            ms = [jnp.max(s, axis=-1, keepdims=True) for s in ss]
            ps = [jnp.exp(s - m) for s, m in zip(ss, ms)]
            for (rows, prows, _), m, p in zip(where, ms, ps):
                store(0, rows, jnp.dot(p.astype(_BF16), values(rows, prows), preferred_element_type=_F32))
                store(1, rows, jnp.broadcast_to(m, (kb, HEAD_DIM)))
                store(2, rows, jnp.broadcast_to(jnp.sum(p, axis=-1, keepdims=True), (kb, HEAD_DIM)))
            return carry

        lax.fori_loop(0, n_blocks // ATTN_UNROLL, blocks, 0)

    stats = (acc_ref, m_ref, l_ref)
    for g, (_, dil) in enumerate(GROUPS):
        src = qkv_refs[3 * g:3 * g + 3]

        def store_tok(a, rows, val, g=g):
            stats[a][g, rows, :] = val

        if dil <= ATTN_MAX_STRIDE:
            run_group(lambda a, rows, src=src: src[a][rows, :], store_tok, dil, seq)
            continue
        inner = ATTN_MAX_STRIDE
        outer = dil // inner
        assert dil == inner * outer
        seg_len = seq // outer
        for a in range(3):
            for r1 in range(outer):
                for c in range(seg_len // kb):
                    stage_in_ref[a, pl.ds(r1 * seg_len + c * kb, kb), :] = (
                        src[a][strided(r1 + c * kb * outer, kb, outer), :])

        def store_seg(a, rows, val):
            stage_out_ref[a, rows, :] = val

        run_group(lambda a, rows: stage_in_ref[a, rows, :], store_seg, inner, seg_len)
        for a in range(3):
            for r1 in range(outer):
                for c in range(seg_len // kb):
                    stats[a][g, strided(r1 + c * kb * outer, kb, outer), :] = (
                        stage_out_ref[a, pl.ds(r1 * seg_len + c * kb, kb), :])

    def merge(c, carry):
        t0 = pl.multiple_of(c * kb, kb)
        ms = [m_ref[g, pl.ds(t0, kb), :] for g in range(len(GROUPS))]
        mx = functools.reduce(jnp.maximum, ms)
        num = jnp.zeros((kb, HEAD_DIM), _F32)
        den = jnp.zeros((kb, HEAD_DIM), _F32)
        for g in range(len(GROUPS)):
            w = jnp.exp(ms[g] - mx)
            num = num + w * acc_ref[g, pl.ds(t0, kb), :]
            den = den + w * l_ref[g, pl.ds(t0, kb), :]
        o_ref[pl.ds(t0, kb), :] = (num / den).astype(o_ref.dtype)
        return carry

    lax.fori_loop(0, seq // kb, merge, 0)


def attn_prompt(qkv, side_roll):
    n_slabs, bsz, seq, _ = qkv.shape
    ng = len(GROUPS)
    n_heads = n_slabs // (3 * ng)

    def spec(section):
        return pl.BlockSpec((None, None, seq, HEAD_DIM), lambda b, h: (section * n_heads + h, b, 0, 0))

    tl = _roll_rows(side_roll, bsz * n_heads)
    cur, nxt, n_side = _side_roll_specs(side_roll, tl, n_heads)
    return pl.pallas_call(
        functools.partial(_attn_prompt_kernel, seq=seq, n_side=n_side, n_heads=n_heads),
        out_shape=(jax.ShapeDtypeStruct((bsz, seq, n_heads * HEAD_DIM), _BF16),
                   jax.ShapeDtypeStruct(side_roll.shape, side_roll.dtype)),
        grid=(bsz, n_heads),
        in_specs=[spec(s) for s in range(3 * ng)] + [cur, nxt],
        out_specs=(pl.BlockSpec((None, seq, HEAD_DIM), lambda b, h: (b, 0, h)), cur),
        scratch_shapes=[pltpu.VMEM((ng, seq, HEAD_DIM), _F32)] * 3 + [pltpu.VMEM((3, seq, HEAD_DIM), _F32)] * 2,
        compiler_params=_params("arbitrary", "arbitrary"),
        name="attn_prompt",
    )(*([qkv] * (3 * ng)), side_roll, side_roll)


def _attn_sample_kernel(*refs):
    ng = len(GROUPS)
    q_refs = refs[0:ng]
    knew_refs = refs[ng:2 * ng]
    vnew_refs = refs[2 * ng:3 * ng]
    kc_refs = refs[3 * ng:4 * ng]
    vc_refs = refs[4 * ng:5 * ng]
    o_ref = refs[5 * ng]
    scale = HEAD_DIM ** -0.5
    parts = []
    for g in range(ng):
        q = q_refs[g][...]
        s_new = jnp.sum(q * knew_refs[g][...], axis=-1, keepdims=True) * scale
        s_old = jnp.sum(q[None] * kc_refs[g][...], axis=-1, keepdims=True) * scale
        m = jnp.maximum(jnp.max(s_old, axis=0), s_new)
        p_new = jnp.exp(s_new - m)
        p_old = jnp.exp(s_old - m[None])
        l = p_new + jnp.sum(p_old, axis=0)
        acc = p_new * vnew_refs[g][...] + jnp.sum(p_old * vc_refs[g][...], axis=0)
        parts.append((m, l, acc))
    mx = functools.reduce(jnp.maximum, [p[0] for p in parts])
    num = sum(jnp.exp(m - mx) * acc for m, _, acc in parts)
    den = sum(jnp.exp(m - mx) * l for m, l, _ in parts)
    o_ref[...] = (num / den).astype(o_ref.dtype)


def attn_sample(qkv, caches, n_heads):
    bsz = qkv.shape[0]
    ng = len(GROUPS)
    kb = KEYS_PER_BLOCK

    def new_spec(section):
        return pl.BlockSpec((None, None, n_heads, HEAD_DIM), lambda b: (b, section, 0, 0))

    def cache_spec(kv):
        return pl.BlockSpec((None, kb, None, None, n_heads, HEAD_DIM), lambda b: (b, 0, 0, kv, 0, 0))

    strided = []
    for (win, dil), cache in zip(GROUPS, caches):
        assert cache.shape[1] == win == kb * dil
        strided.append(cache.reshape(bsz, kb, dil, 2, n_heads, HEAD_DIM))
    in_specs = ([new_spec(3 * g) for g in range(ng)] + [new_spec(3 * g + 1) for g in range(ng)]
                + [new_spec(3 * g + 2) for g in range(ng)] + [cache_spec(0)] * ng + [cache_spec(1)] * ng)
    return pl.pallas_call(
        _attn_sample_kernel,
        out_shape=jax.ShapeDtypeStruct((bsz, n_heads, HEAD_DIM), _BF16),
        grid=(bsz,),
        in_specs=in_specs,
        out_specs=pl.BlockSpec((None, n_heads, HEAD_DIM), lambda b: (b, 0, 0)),
        compiler_params=_params("arbitrary"),
        name="attn_sample",
    )(*([qkv] * (3 * ng)), *strided, *strided)


def _set_last_kernel(rolled_ref, new_ref, o_ref):
    del rolled_ref
    o_ref[...] = new_ref[...]


def cache_set_last(rolled, new_row):
    bsz, length = rolled.shape[:2]
    tail = rolled.shape[2:]
    zeros = (0,) * len(tail)
    return pl.pallas_call(
        _set_last_kernel,
        out_shape=jax.ShapeDtypeStruct(rolled.shape, rolled.dtype),
        grid=(bsz,),
        in_specs=[pl.BlockSpec(memory_space=pl.ANY),
                  pl.BlockSpec((None, 1) + tail, lambda b: (b, 0) + zeros)],
        out_specs=pl.BlockSpec((None, 1) + tail, lambda b: (b, length - 1) + zeros),
        input_output_aliases={0: 0},
        compiler_params=_params("arbitrary"),
        name="cache_set_last",
    )(rolled, new_row)


def _pad_rows(a, rows):
    return jnp.pad(a, ((0, rows - a.shape[0]),) + ((0, 0),) * (a.ndim - 1))


def kernel(x_prompt, x_sample, state_conv, state_h, cache_kv_w128, cache_kv_w512, cache_kv_w2048,
           lru_w_in, lru_b_in, lru_conv_w, lru_conv_b, lru_w_ra, lru_b_ra, lru_w_ix, lru_b_ix,
           lru_lambda, lru_w_out, lru_b_out, attn_w_qkv, attn_w_o,
           mlp_w1, mlp_b1, mlp_w2, mlp_b2, ln_g, ln_b):
    bp, seq, d = x_prompt.shape
    bs, dec_seq, _ = x_sample.shape
    assert dec_seq == 1
    n_heads = d // HEAD_DIM
    ng = len(GROUPS)
    mp = bp * seq
    ms = -(-bs // V7X_BF16_SUBLANES) * V7X_BF16_SUBLANES
    caches = (cache_kv_w128[0], cache_kv_w512[0], cache_kv_w2048[0])
    row = lambda v: v[:, None, :]
    b_in, conv_b, b_ra, b_ix, lam, b_out = (row(v) for v in (lru_b_in, lru_conv_b, lru_b_ra, lru_b_ix,
                                                               lru_lambda, lru_b_out))
    b1, b2 = row(mlp_b1), row(mlp_b2)
    g_ln, b_ln = ln_g[:, :, None, :], ln_b[:, :, None, :]

    xp = x_prompt.reshape(mp, d)
    xs = _pad_rows(x_sample.reshape(bs, d), ms)

    def norm(xf, y, i, k):
        return deepnorm(xf, y, g_ln[i][k], b_ln[i][k])

    def mlp(xfp, xbp, xfs, xbs, i, side_rolls=()):
        hp, hs, *rolled = matmul(xbp, xbs, mlp_w1, i, b1[i], act="relu2", out_dtype=_BF16,
                                 tiled_out=True, side_rolls=side_rolls)
        yp, ys = matmul_ktiled(hp, hs, mlp_w2, i, b2[i])
        return norm(xfp, yp, i, 1), norm(xfs, ys, i, 1), rolled

    d_rnn = lru_conv_w.shape[2]
    up, us = matmul(xp.astype(_BF16), xs.astype(_BF16), lru_w_in, 0, b_in[0],
                    act="gelu", act_cols=d_rnn)
    lru_args = (lru_conv_w[0], conv_b[0], lru_w_ra[0], b_ra[0], lru_w_ix[0], b_ix[0], lam[0])
    up3 = up.reshape(bp, seq, 2 * d_rnn)
    hgp, h_p = rglru_prompt(up3, *lru_args)
    conv_p = up3[:, seq - (CONV_W - 1):, d_rnn:]
    conv_prev = [_pad_rows(state_conv[0][:, k], ms) for k in range(CONV_W - 1)]
    hgs, h_s = rglru_step(us, conv_prev, _pad_rows(state_h[0], ms), *lru_args)
    conv_s = jnp.concatenate([state_conv[0][:, 1:], us[:bs, None, d_rnn:]], axis=1)
    yp, ys = matmul(hgp.reshape(mp, d_rnn), hgs, lru_w_out, 0, b_out[0])
    (xfp, xbp), (xfs, xbs) = norm(xp, yp, 0, 0), norm(xs, ys, 0, 0)
    (xfp, xbp), (xfs, xbs), _ = mlp(xfp, xbp, xfs, xbs, 0)

    zero_bias = jnp.zeros((1, attn_w_qkv.shape[2]), _F32)
    qkv_p, qkv_s = matmul(xbp, xbs, attn_w_qkv, 0, zero_bias, slabs=True)
    qkv_p = qkv_p.reshape(3 * ng * n_heads, bp, seq, HEAD_DIM)
    op, rolled_big = attn_prompt(qkv_p, caches[-1])
    op = op.reshape(mp, d)
    kv_p = []
    for g, (win, _) in enumerate(GROUPS):
        keep = min(win, seq)
        kv = qkv_p[(3 * g + 1) * n_heads:(3 * g + 3) * n_heads, :, seq - keep:]
        kv_p.append(jnp.transpose(kv, (1, 2, 0, 3)).reshape(1, bp, keep, 2, n_heads, HEAD_DIM))
    qkv_s = jnp.transpose(qkv_s[:, :bs], (1, 0, 2)).reshape(bs, 3 * ng, n_heads, HEAD_DIM)
    os_ = _pad_rows(attn_sample(qkv_s, caches, n_heads).reshape(bs, d), ms)
    yp, ys = matmul(op, os_, attn_w_o, 0, jnp.zeros((1, d), _F32))
    (xfp, xbp), (xfs, xbs) = norm(xfp, yp, 1, 0), norm(xfs, ys, 1, 0)
    (xfp, _), (xfs, _), rolled_small = mlp(xfp, xbp, xfs, xbs, 1, side_rolls=caches[:-1])

    rolled = (*rolled_small, rolled_big)
    kv_s = []
    for g in range(ng):
        new_row = qkv_s[:, 3 * g + 1:3 * g + 3].reshape(bs, 1, 2, n_heads, HEAD_DIM)
        kv_s.append(cache_set_last(rolled[g], new_row)[None])

    y_p = xfp.reshape(bp, seq, d)
    y_s = xfs[:bs].reshape(bs, 1, d)
    return (y_p, y_s, conv_p[None], h_p.reshape(1, bp, d_rnn), kv_p[0], kv_p[1], kv_p[2],
            conv_s[None], h_s[:bs][None], kv_s[0], kv_s[1], kv_s[2])
```

```python
import functools

import jax
import jax.numpy as jnp
from jax import lax
from jax.experimental import pallas as pl
from jax.experimental.pallas import tpu as pltpu

HEAD_DIM = 128
CONV_W = 4
LRU_C = 8.0
GROUPS = ((128, 1), (512, 4), (2048, 16))
KEYS_PER_BLOCK = 128
LN_EPS = 1e-5
DEPTH = 2
ALPHA = (2.0 * DEPTH) ** 0.25

V7X_SUBLANES = 8
V7X_BF16_SUBLANES = 16
V7X_VMEM_LIMIT_BYTES = 62 * 1024 * 1024

MM_TILE = 1024
MM_FULL_K = 4096
MM_KACC_TILE_M = 2048
MM_KACC_SUBTILES = 2
ATTN_UNROLL = 8
ATTN_MAX_STRIDE = 4
SIDE_ROLL_MIN_ROWS = 32
LN_ROWS = 256
SCAN_CHUNK = 256
SCAN_SUB = 64

_BF16 = jnp.bfloat16
_F32 = jnp.float32


def _params(*sem):
    return pltpu.CompilerParams(dimension_semantics=sem, vmem_limit_bytes=V7X_VMEM_LIMIT_BYTES)


def _apply_act(acc, act):
    if act == "gelu":
        return jax.nn.gelu(acc)
    if act == "relu2":
        r = jnp.maximum(acc, 0.0)
        return r * r
    return acc


def _roll_block(cur_ref, nxt_ref, out_ref):
    tl = out_ref.shape[0]
    out_ref[pl.ds(0, tl - 1)] = cur_ref[pl.ds(1, tl - 1)]
    out_ref[pl.ds(tl - 1, 1)] = nxt_ref[...]


def _mm_kernel(*refs, act, act_tiles, ck, nj, ni, side_blocks, cast_blocks, slabs):
    ns = len(side_blocks)
    nc = 1 if cast_blocks else 0
    x_ref, wc_ref, b_ref, xs_ref = refs[:4]
    side_in = refs[4:4 + 2 * ns]
    cast_in = refs[4 + 2 * ns:4 + 2 * ns + nc]
    outs = refs[4 + 2 * ns + nc:]
    o_ref, os_ref = outs[:2]
    side_out = outs[2:2 + ns]
    cast_out = outs[2 + ns:2 + ns + nc]
    wbf_ref = outs[2 + ns + nc]
    j = pl.program_id(0)
    i = pl.program_id(1)

    for c, n_side in enumerate(side_blocks):
        @pl.when(j * ni + i < n_side)
        def _(c=c):
            _roll_block(side_in[2 * c], side_in[2 * c + 1], side_out[c])

    if cast_blocks:
        @pl.when(j * ni + i < cast_blocks)
        def _():
            cast_out[0][...] = cast_in[0][...].astype(_BF16)

    @pl.when(j < nj)
    def _():
        r0 = pl.multiple_of(i * ck, ck)
        wbf_ref[j % 2, pl.ds(r0, ck), :] = wc_ref[...].astype(_BF16)

    def put(out_ref, val):
        if not slabs:
            out_ref[...] = val.astype(out_ref.dtype)
            return
        for h in range(out_ref.shape[0]):
            out_ref[h] = val[:, h * HEAD_DIM:(h + 1) * HEAD_DIM].astype(out_ref.dtype)

    def emit(lhs_ref, out_ref):
        acc = jnp.dot(lhs_ref[...], wbf_ref[(j - 1) % 2], preferred_element_type=_F32) + b_ref[...]
        if act_tiles is None:
            put(out_ref, _apply_act(acc, act))
        else:
            @pl.when(j - 1 < act_tiles)
            def _():
                put(out_ref, _apply_act(acc, act))

            @pl.when(j - 1 >= act_tiles)
            def _():
                put(out_ref, acc)

    @pl.when(j > 0)
    def _():
        emit(x_ref, o_ref)

    @pl.when(jnp.logical_and(j > 0, i == 0))
    def _():
        emit(xs_ref, os_ref)


def _mm_acc_kernel(*refs, n_sub, ni, nk, side_blocks):
    ns = len(side_blocks)
    x_ref, w_ref, b_ref, xs_ref = refs[:4]
    side_in = refs[4:4 + 2 * ns]
    o_ref, os_ref = refs[4 + 2 * ns:6 + 2 * ns]
    side_out = refs[6 + 2 * ns:]
    i = pl.program_id(1)
    k = pl.program_id(2)

    for c, n_side in enumerate(side_blocks):
        @pl.when((pl.program_id(0) * ni + i) * nk + k < n_side)
        def _(c=c):
            _roll_block(side_in[2 * c], side_in[2 * c + 1], side_out[c])

    @pl.when(k == 0)
    def _():
        o_ref[...] = jnp.broadcast_to(b_ref[...], o_ref.shape)

    sub = w_ref.shape[0] // n_sub

    def contract(lhs_ref):
        return sum(jnp.dot(lhs_ref[s], w_ref[pl.ds(s * sub, sub), :], preferred_element_type=_F32)
                   for s in range(n_sub))

    o_ref[...] += contract(x_ref)

    @pl.when(jnp.logical_and(i == 0, k == 0))
    def _():
        os_ref[...] = jnp.broadcast_to(b_ref[...], os_ref.shape)

    @pl.when(i == 0)
    def _():
        os_ref[...] += contract(xs_ref)


def _roll_rows(cache, steps):
    bsz, length = cache.shape[:2]
    tl = min(SIDE_ROLL_MIN_ROWS, length)
    while bsz * (length // tl) > steps:
        assert tl < length
        tl *= 2
    return tl


def _side_roll_specs(cache, tl, step):
    bsz, length = cache.shape[:2]
    tail = cache.shape[2:]
    assert length % tl == 0
    per_batch = length // tl
    n_side = bsz * per_batch
    zeros = (0,) * len(tail)

    def where(*ids):
        t = jnp.minimum(step(*ids), n_side - 1)
        return t // per_batch, t % per_batch

    def cur_map(*ids):
        return where(*ids) + zeros

    def nxt_map(*ids):
        bi, ti = where(*ids)
        return (bi, jnp.minimum((ti + 1) * tl, length - 1)) + zeros

    cur = pl.BlockSpec((None, tl) + tail, cur_map)
    nxt = pl.BlockSpec((None, 1) + tail, nxt_map)
    return cur, nxt, n_side


def matmul(x, xs, w, layer, b, *, act=None, act_cols=None, out_dtype=_F32, side_rolls=(), side_cast=None,
           tiled_out=False, slabs=False):
    m, kdim = x.shape
    ms = xs.shape[0]
    n = w.shape[2]
    tm = min(MM_TILE, m)
    tn = min(MM_TILE, n)
    assert kdim <= MM_FULL_K and m % tm == 0 and n % tn == 0
    ni, nj = m // tm, n // tn
    assert kdim % ni == 0
    ck = kdim // ni
    act_tiles = None
    if act_cols is not None:
        assert act_cols % tn == 0
        act_tiles = act_cols // tn
    side_in_specs, side_out_specs, side_blocks, side_args = [], [], [], []
    for cache in side_rolls:
        tl = _roll_rows(cache, (nj + 1) * ni)
        cur, nxt, n_side = _side_roll_specs(cache, tl, lambda j, i: j * ni + i)
        side_in_specs += [cur, nxt]
        side_out_specs.append(cur)
        side_blocks.append(n_side)
        side_args += [cache, cache]
    cast_blocks, cast_in_specs, cast_out_specs, cast_shapes = 0, [], [], []
    if side_cast is not None:
        w2, layer2 = side_cast
        _, k2, n2 = w2.shape
        cast_blocks = 1
        while cast_blocks * 2 <= (nj + 1) * ni and k2 % (cast_blocks * 2 * V7X_BF16_SUBLANES) == 0:
            cast_blocks *= 2
        rows2 = k2 // cast_blocks
        blk2 = lambda j, i: jnp.minimum(j * ni + i, cast_blocks - 1)
        cast_in_specs.append(pl.BlockSpec((None, rows2, n2), lambda j, i: (layer2, blk2(j, i), 0)))
        cast_out_specs.append(pl.BlockSpec((rows2, n2), lambda j, i: (blk2(j, i), 0)))
        cast_shapes.append(jax.ShapeDtypeStruct((k2, n2), _BF16))
        side_args.append(w2)
    row_tile = lambda j, i: jnp.where(j == 0, 0, i)
    col_tile = lambda j, i: jnp.maximum(j - 1, 0)
    assert not (tiled_out and slabs)
    if tiled_out:
        shapes = [(nj, m, tn), (nj, ms, tn)]
        o_spec = pl.BlockSpec((None, tm, tn), lambda j, i: (col_tile(j, i), row_tile(j, i), 0))
        os_spec = pl.BlockSpec((None, ms, tn), lambda j, i: (col_tile(j, i), 0, 0))
    elif slabs:
        per_tile = tn // HEAD_DIM
        shapes = [(n // HEAD_DIM, m, HEAD_DIM), (n // HEAD_DIM, ms, HEAD_DIM)]
        o_spec = pl.BlockSpec((per_tile, tm, HEAD_DIM), lambda j, i: (col_tile(j, i), row_tile(j, i), 0))
        os_spec = pl.BlockSpec((per_tile, ms, HEAD_DIM), lambda j, i: (col_tile(j, i), 0, 0))
    else:
        shapes = [(m, n), (ms, n)]
        o_spec = pl.BlockSpec((tm, tn), lambda j, i: (row_tile(j, i), col_tile(j, i)))
        os_spec = pl.BlockSpec((ms, tn), lambda j, i: (0, col_tile(j, i)))
    return pl.pallas_call(
        functools.partial(_mm_kernel, act=act, act_tiles=act_tiles, ck=ck, nj=nj, ni=ni,
                          side_blocks=tuple(side_blocks), cast_blocks=cast_blocks, slabs=slabs),
        out_shape=(*[jax.ShapeDtypeStruct(s, out_dtype) for s in shapes],
                   *[jax.ShapeDtypeStruct(c.shape, c.dtype) for c in side_rolls], *cast_shapes),
        grid=(nj + 1, ni),
        in_specs=[
            pl.BlockSpec((tm, kdim), lambda j, i: (row_tile(j, i), 0)),
            pl.BlockSpec((None, ck, tn),
                         lambda j, i: (layer, jnp.where(j < nj, i, ni - 1), jnp.minimum(j, nj - 1))),
            pl.BlockSpec((1, tn), lambda j, i: (0, col_tile(j, i))),
            pl.BlockSpec((ms, kdim), lambda j, i: (0, 0)),
            *side_in_specs, *cast_in_specs,
        ],
        out_specs=(o_spec, os_spec, *side_out_specs, *cast_out_specs),
        scratch_shapes=[pltpu.VMEM((2, kdim, tn), _BF16)],
        compiler_params=_params("arbitrary", "arbitrary"),
        name="matmul",
    )(x, w, b, xs, *side_args)


def matmul_ktiled(x, xs, w, b, side_rolls=()):
    nkt, m, tk = x.shape
    ms = xs.shape[1]
    n = w.shape[1]
    tm = min(MM_KACC_TILE_M, m)
    tn = min(MM_TILE, n)
    n_sub = min(MM_KACC_SUBTILES, nkt)
    assert m % tm == 0 and n % tn == 0 and nkt % n_sub == 0 and w.shape[0] == nkt * tk
    nj, ni, nk = n // tn, m // tm, nkt // n_sub
    side_in_specs, side_out_specs, side_blocks, side_args = [], [], [], []
    for cache in side_rolls:
        tl = _roll_rows(cache, nj * ni * nk)
        cur, nxt, n_side = _side_roll_specs(cache, tl, lambda j, i, k: (j * ni + i) * nk + k)
        side_in_specs += [cur, nxt]
        side_out_specs.append(cur)
        side_blocks.append(n_side)
        side_args += [cache, cache]
    return pl.pallas_call(
        functools.partial(_mm_acc_kernel, n_sub=n_sub, ni=ni, nk=nk, side_blocks=tuple(side_blocks)),
        out_shape=(jax.ShapeDtypeStruct((m, n), _F32), jax.ShapeDtypeStruct((ms, n), _F32),
                   *[jax.ShapeDtypeStruct(c.shape, c.dtype) for c in side_rolls]),
        grid=(nj, ni, nk),
        in_specs=[
            pl.BlockSpec((n_sub, tm, tk), lambda j, i, k: (k, i, 0)),
            pl.BlockSpec((n_sub * tk, tn), lambda j, i, k: (k, j)),
            pl.BlockSpec((1, tn), lambda j, i, k: (0, j)),
            pl.BlockSpec((n_sub, ms, tk), lambda j, i, k: (k, 0, 0)),
            *side_in_specs,
        ],
        out_specs=(pl.BlockSpec((tm, tn), lambda j, i, k: (i, j)),
                   pl.BlockSpec((ms, tn), lambda j, i, k: (0, j)), *side_out_specs),
        compiler_params=_params("arbitrary", "arbitrary", "arbitrary"),
        name="matmul_ktiled",
    )(x, w, b, xs, *side_args)


def _ln_kernel(x_ref, y_ref, g_ref, b_ref, o_ref, obf_ref):
    z = ALPHA * x_ref[...] + y_ref[...]
    mu = jnp.mean(z, axis=-1, keepdims=True)
    zc = z - mu
    var = jnp.mean(zc * zc, axis=-1, keepdims=True)
    out = zc * lax.rsqrt(var + LN_EPS) * g_ref[...] + b_ref[...]
    o_ref[...] = out
    obf_ref[...] = out.astype(_BF16)


def deepnorm(x, y, g, b):
    m, d = x.shape
    tm = min(LN_ROWS, m)
    assert m % tm == 0
    row = pl.BlockSpec((tm, d), lambda i: (i, 0))
    vec = pl.BlockSpec((1, d), lambda i: (0, 0))
    return pl.pallas_call(
        _ln_kernel,
        out_shape=(jax.ShapeDtypeStruct((m, d), _F32), jax.ShapeDtypeStruct((m, d), _BF16)),
        grid=(m // tm,),
        in_specs=[row, row, vec, vec],
        out_specs=(row, row),
        compiler_params=_params("arbitrary"),
        name="deepnorm",
    )(x, y, g, b)


def _softplus(x):
    return jnp.maximum(x, 0.0) + jnp.log1p(jnp.exp(-jnp.abs(x)))


def _lru_coeffs(xc, wra, bra, wix, bix, sp):
    xcb = xc.astype(_BF16)
    r = jax.nn.sigmoid(jnp.dot(xcb, wra, preferred_element_type=_F32) + bra)
    i = jax.nn.sigmoid(jnp.dot(xcb, wix, preferred_element_type=_F32) + bix)
    log_a = -LRU_C * r * sp
    a = jnp.exp(log_a)
    t = jnp.tanh(log_a)
    bt = jnp.sqrt(-2.0 * t / (1.0 - t)) * (i * xc)
    return a, bt


def _shift_rows(x, s, fill):
    rows = lax.broadcasted_iota(jnp.int32, x.shape, 0)
    return jnp.where(rows < s, fill, pltpu.roll(x, s, 0))


def _rglru_kernel(xb_ref, gate_ref, cw_ref, cb_ref, wra_ref, bra_ref, wix_ref, bix_ref, lam_ref,
                  o_ref, hlast_ref, xpad_ref, *, seq, chunk):
    pad = V7X_SUBLANES
    cw = cw_ref[...]
    cbias = cb_ref[...]
    wra = wra_ref[...].astype(_BF16)
    wix = wix_ref[...].astype(_BF16)
    bra = bra_ref[...]
    bix = bix_ref[...]
    sp = _softplus(-lam_ref[...])

    xpad_ref[pl.ds(0, pad), :] = jnp.zeros((pad, xpad_ref.shape[1]), _F32)
    xpad_ref[pl.ds(pad, seq), :] = xb_ref[...]

    def body(c, h):
        t0 = pl.multiple_of(c * chunk, chunk)
        xw = xpad_ref[pl.ds(t0, chunk + pad), :]
        xc = cbias
        for k in range(CONV_W):
            off = pad - (CONV_W - 1) + k
            xc = xc + cw[k:k + 1, :] * xw[off:off + chunk, :]
        a, bt = _lru_coeffs(xc, wra, bra, wix, bix, sp)
        hs = []
        for q in range(chunk // SCAN_SUB):
            aq = a[q * SCAN_SUB:(q + 1) * SCAN_SUB]
            bq = bt[q * SCAN_SUB:(q + 1) * SCAN_SUB]
            s = 1
            while s < SCAN_SUB:
                if s < V7X_SUBLANES:
                    a_sh = _shift_rows(aq, s, 1.0)
                    b_sh = _shift_rows(bq, s, 0.0)
                    bq = aq * b_sh + bq
                    aq = aq * a_sh
                else:
                    bq = jnp.concatenate([bq[:s], aq[s:] * bq[:-s] + bq[s:]], axis=0)
                    aq = jnp.concatenate([aq[:s], aq[s:] * aq[:-s]], axis=0)
                s *= 2
            hq = aq * h + bq
            h = hq[SCAN_SUB - 1:SCAN_SUB, :]
            hs.append(hq)
        hs = jnp.concatenate(hs, axis=0)
        o_ref[pl.ds(t0, chunk), :] = (hs * gate_ref[pl.ds(t0, chunk), :]).astype(o_ref.dtype)
        return h

    h0 = jnp.zeros((1, xb_ref.shape[1]), _F32)
    hlast_ref[...] = lax.fori_loop(0, seq // chunk, body, h0)


def rglru_prompt(u, cw, cb, wra, bra, wix, bix, lam):
    bsz, seq, c2 = u.shape
    c = c2 // 2
    nblk, blk, _ = wra.shape
    assert c == nblk * blk and seq % SCAN_CHUNK == 0
    col = lambda b, n: (b, 0, n)
    vec = pl.BlockSpec((1, blk), lambda b, n: (0, n))
    wspec = pl.BlockSpec((None, blk, blk), lambda b, n: (n, 0, 0))
    return pl.pallas_call(
        functools.partial(_rglru_kernel, seq=seq, chunk=SCAN_CHUNK),
        out_shape=(jax.ShapeDtypeStruct((bsz, seq, c), _BF16), jax.ShapeDtypeStruct((bsz, 1, c), _F32)),
        grid=(bsz, nblk),
        in_specs=[
            pl.BlockSpec((None, seq, blk), lambda b, n: (b, 0, n + nblk)),
            pl.BlockSpec((None, seq, blk), col),
            pl.BlockSpec((CONV_W, blk), lambda b, n: (0, n)),
            vec, wspec, vec, wspec, vec, vec,
        ],
        out_specs=(pl.BlockSpec((None, seq, blk), col), pl.BlockSpec((None, 1, blk), col)),
        scratch_shapes=[pltpu.VMEM((seq + V7X_SUBLANES, blk), _F32)],
        compiler_params=_params("arbitrary", "arbitrary"),
        name="rglru_prompt",
    )(u, u, cw, cb, wra, bra, wix, bix, lam)


def _rglru_step_kernel(xb_ref, gate_ref, c0_ref, c1_ref, c2_ref, h_ref, cw_ref, cb_ref,
                       wra_ref, bra_ref, wix_ref, bix_ref, lam_ref, o_ref, hnew_ref):
    cw = cw_ref[...]
    xc = (cb_ref[...] + cw[0:1, :] * c0_ref[...] + cw[1:2, :] * c1_ref[...]
          + cw[2:3, :] * c2_ref[...] + cw[3:4, :] * xb_ref[...])
    a, bt = _lru_coeffs(xc, wra_ref[...].astype(_BF16), bra_ref[...], wix_ref[...].astype(_BF16),
                        bix_ref[...], _softplus(-lam_ref[...]))
    h = a * h_ref[...] + bt
    hnew_ref[...] = h
    o_ref[...] = (h * gate_ref[...]).astype(o_ref.dtype)


def rglru_step(u, conv_prev, h_prev, cw, cb, wra, bra, wix, bix, lam):
    rows, c2 = u.shape
    c = c2 // 2
    nblk, blk, _ = wra.shape
    col = pl.BlockSpec((rows, blk), lambda n: (0, n))
    vec = pl.BlockSpec((1, blk), lambda n: (0, n))
    wspec = pl.BlockSpec((None, blk, blk), lambda n: (n, 0, 0))
    return pl.pallas_call(
        _rglru_step_kernel,
        out_shape=(jax.ShapeDtypeStruct((rows, c), _BF16), jax.ShapeDtypeStruct((rows, c), _F32)),
        grid=(nblk,),
        in_specs=[pl.BlockSpec((rows, blk), lambda n: (0, n + nblk)), col, col, col, col, col,
                  pl.BlockSpec((CONV_W, blk), lambda n: (0, n)), vec, wspec, vec, wspec, vec, vec],
        out_specs=(col, col),
        compiler_params=_params("arbitrary"),
        name="rglru_step",
    )(u, u, *conv_prev, h_prev, cw, cb, wra, bra, wix, bix, lam)


def _attn_prompt_kernel(*refs, seq):
    qkv_refs = refs[:9]
    o_ref = refs[9]
    acc_ref, m_ref, l_ref, stage_in_ref, stage_out_ref = refs[10:15]
    kb = KEYS_PER_BLOCK
    scale = HEAD_DIM ** -0.5
    qi = lax.broadcasted_iota(jnp.int32, (kb, kb), 0)
    ki = lax.broadcasted_iota(jnp.int32, (kb, kb), 1)
    cur_mask = ki <= qi
    qi2 = lax.broadcasted_iota(jnp.int32, (kb, 2 * kb), 0)
    ki2 = lax.broadcasted_iota(jnp.int32, (kb, 2 * kb), 1)
    cur_part = jnp.logical_and(ki2 >= kb, ki2 - kb <= qi2)
    prev_part = jnp.logical_and(ki2 < kb, ki2 >= qi2)
    n_blocks = seq // kb
    assert n_blocks % ATTN_UNROLL == 0

    def strided(start, n, st):
        return pl.ds(start, n) if st == 1 else pl.ds(start, n, stride=st)

    def run_group(load, store, st, seg_len):
        per_res = seg_len // st
        nblk = per_res // kb
        assert per_res % kb == 0 and n_blocks % (st * nblk) == 0

        def block_rows(idx):
            seg = idx // (st * nblk)
            r = (idx // nblk) % st
            j = idx % nblk
            first = seg * seg_len + r
            start = first + j * (kb * st)
            prows = strided(jnp.maximum(start - kb * st, first), kb, st)
            return strided(start, kb, st), prows, j

        def scores(rows, prows, j):
            q = load(0, rows).astype(_BF16)
            k = load(1, rows).astype(_BF16)
            if nblk > 1:
                k = jnp.concatenate([load(1, prows).astype(_BF16), k], axis=0)
                mask = jnp.logical_or(cur_part, jnp.logical_and(prev_part, j > 0))
            else:
                mask = cur_mask
            s = lax.dot_general(q, k, (((1,), (1,)), ((), ())), preferred_element_type=_F32) * scale
            return jnp.where(mask, s, -jnp.inf)

        def values(rows, prows):
            v = load(2, rows).astype(_BF16)
            if nblk > 1:
                v = jnp.concatenate([load(2, prows).astype(_BF16), v], axis=0)
            return v

        def blocks(it, carry):
            where = [block_rows(it * ATTN_UNROLL + u) for u in range(ATTN_UNROLL)]
            ss = [scores(rows, prows, j) for rows, prows, j in where]
            ms = [jnp.max(s, axis=-1, keepdims=True) for s in ss]
            ps = [jnp.exp(s - m) for s, m in zip(ss, ms)]
            for (rows, prows, _), m, p in zip(where, ms, ps):
                store(0, rows, jnp.dot(p.astype(_BF16), values(rows, prows), preferred_element_type=_F32))
                store(1, rows, jnp.broadcast_to(m, (kb, HEAD_DIM)))
                store(2, rows, jnp.broadcast_to(jnp.sum(p, axis=-1, keepdims=True), (kb, HEAD_DIM)))
            return carry

        lax.fori_loop(0, n_blocks // ATTN_UNROLL, blocks, 0)

    stats = (acc_ref, m_ref, l_ref)
    for g, (_, dil) in enumerate(GROUPS):
        src = qkv_refs[3 * g:3 * g + 3]

        def store_tok(a, rows, val, g=g):
            stats[a][g, rows, :] = val

        if dil <= ATTN_MAX_STRIDE:
            run_group(lambda a, rows, src=src: src[a][rows, :], store_tok, dil, seq)
            continue
        inner = ATTN_MAX_STRIDE
        outer = dil // inner
        assert dil == inner * outer
        seg_len = seq // outer
        for a in range(3):
            for r1 in range(outer):
                for c in range(seg_len // kb):
                    stage_in_ref[a, pl.ds(r1 * seg_len + c * kb, kb), :] = (
                        src[a][strided(r1 + c * kb * outer, kb, outer), :])

        def store_seg(a, rows, val):
            stage_out_ref[a, rows, :] = val

        run_group(lambda a, rows: stage_in_ref[a, rows, :], store_seg, inner, seg_len)
        for a in range(3):
            for r1 in range(outer):
                for c in range(seg_len // kb):
                    stats[a][g, strided(r1 + c * kb * outer, kb, outer), :] = (
                        stage_out_ref[a, pl.ds(r1 * seg_len + c * kb, kb), :])

    def merge(c, carry):
        t0 = pl.multiple_of(c * kb, kb)
        ms = [m_ref[g, pl.ds(t0, kb), :] for g in range(len(GROUPS))]
        mx = functools.reduce(jnp.maximum, ms)
        num = jnp.zeros((kb, HEAD_DIM), _F32)
        den = jnp.zeros((kb, HEAD_DIM), _F32)
        for g in range(len(GROUPS)):
            w = jnp.exp(ms[g] - mx)
            num = num + w * acc_ref[g, pl.ds(t0, kb), :]
            den = den + w * l_ref[g, pl.ds(t0, kb), :]
        o_ref[pl.ds(t0, kb), :] = (num / den).astype(o_ref.dtype)
        return carry

    lax.fori_loop(0, seq // kb, merge, 0)


def attn_prompt(qkv):
    n_slabs, bsz, seq, _ = qkv.shape
    ng = len(GROUPS)
    n_heads = n_slabs // (3 * ng)

    def spec(section):
        return pl.BlockSpec((None, None, seq, HEAD_DIM), lambda b, h: (section * n_heads + h, b, 0, 0))

    return pl.pallas_call(
        functools.partial(_attn_prompt_kernel, seq=seq),
        out_shape=jax.ShapeDtypeStruct((bsz, seq, n_heads * HEAD_DIM), _BF16),
        grid=(bsz, n_heads),
        in_specs=[spec(s) for s in range(3 * ng)],
        out_specs=pl.BlockSpec((None, seq, HEAD_DIM), lambda b, h: (b, 0, h)),
        scratch_shapes=[pltpu.VMEM((ng, seq, HEAD_DIM), _F32)] * 3 + [pltpu.VMEM((3, seq, HEAD_DIM), _F32)] * 2,
        compiler_params=_params("arbitrary", "arbitrary"),
        name="attn_prompt",
    )(*([qkv] * (3 * ng)))


def _attn_sample_kernel(*refs):
    ng = len(GROUPS)
    q_refs = refs[0:ng]
    knew_refs = refs[ng:2 * ng]
    vnew_refs = refs[2 * ng:3 * ng]
    kc_refs = refs[3 * ng:4 * ng]
    vc_refs = refs[4 * ng:5 * ng]
    o_ref = refs[5 * ng]
    scale = HEAD_DIM ** -0.5
    parts = []
    for g in range(ng):
        q = q_refs[g][...]
        s_new = jnp.sum(q * knew_refs[g][...], axis=-1, keepdims=True) * scale
        s_old = jnp.sum(q[None] * kc_refs[g][...], axis=-1, keepdims=True) * scale
        m = jnp.maximum(jnp.max(s_old, axis=0), s_new)
        p_new = jnp.exp(s_new - m)
        p_old = jnp.exp(s_old - m[None])
        l = p_new + jnp.sum(p_old, axis=0)
        acc = p_new * vnew_refs[g][...] + jnp.sum(p_old * vc_refs[g][...], axis=0)
        parts.append((m, l, acc))
    mx = functools.reduce(jnp.maximum, [p[0] for p in parts])
    num = sum(jnp.exp(m - mx) * acc for m, _, acc in parts)
    den = sum(jnp.exp(m - mx) * l for m, l, _ in parts)
    o_ref[...] = (num / den).astype(o_ref.dtype)


def attn_sample(qkv, caches, n_heads):
    bsz = qkv.shape[0]
    ng = len(GROUPS)
    kb = KEYS_PER_BLOCK

    def new_spec(section):
        return pl.BlockSpec((None, None, n_heads, HEAD_DIM), lambda b: (b, section, 0, 0))

    def cache_spec(kv):
        return pl.BlockSpec((None, kb, None, None, n_heads, HEAD_DIM), lambda b: (b, 0, 0, kv, 0, 0))

    strided = []
    for (win, dil), cache in zip(GROUPS, caches):
        assert cache.shape[1] == win == kb * dil
        strided.append(cache.reshape(bsz, kb, dil, 2, n_heads, HEAD_DIM))
    in_specs = ([new_spec(3 * g) for g in range(ng)] + [new_spec(3 * g + 1) for g in range(ng)]
                + [new_spec(3 * g + 2) for g in range(ng)] + [cache_spec(0)] * ng + [cache_spec(1)] * ng)
    return pl.pallas_call(
        _attn_sample_kernel,
        out_shape=jax.ShapeDtypeStruct((bsz, n_heads, HEAD_DIM), _BF16),
        grid=(bsz,),
        in_specs=in_specs,
        out_specs=pl.BlockSpec((None, n_heads, HEAD_DIM), lambda b: (b, 0, 0)),
        compiler_params=_params("arbitrary"),
        name="attn_sample",
    )(*([qkv] * (3 * ng)), *strided, *strided)


def _set_last_kernel(rolled_ref, new_ref, o_ref):
    del rolled_ref
    o_ref[...] = new_ref[...]


def cache_set_last(rolled, new_row):
    bsz, length = rolled.shape[:2]
    tail = rolled.shape[2:]
    zeros = (0,) * len(tail)
    return pl.pallas_call(
        _set_last_kernel,
        out_shape=jax.ShapeDtypeStruct(rolled.shape, rolled.dtype),
        grid=(bsz,),
        in_specs=[pl.BlockSpec(memory_space=pl.ANY),
                  pl.BlockSpec((None, 1) + tail, lambda b: (b, 0) + zeros)],
        out_specs=pl.BlockSpec((None, 1) + tail, lambda b: (b, length - 1) + zeros),
        input_output_aliases={0: 0},
        compiler_params=_params("arbitrary"),
        name="cache_set_last",
    )(rolled, new_row)


def _pad_rows(a, rows):
    return jnp.pad(a, ((0, rows - a.shape[0]),) + ((0, 0),) * (a.ndim - 1))


def kernel(x_prompt, x_sample, state_conv, state_h, cache_kv_w128, cache_kv_w512, cache_kv_w2048,
           lru_w_in, lru_b_in, lru_conv_w, lru_conv_b, lru_w_ra, lru_b_ra, lru_w_ix, lru_b_ix,
           lru_lambda, lru_w_out, lru_b_out, attn_w_qkv, attn_w_o,
           mlp_w1, mlp_b1, mlp_w2, mlp_b2, ln_g, ln_b):
    bp, seq, d = x_prompt.shape
    bs, dec_seq, _ = x_sample.shape
    assert dec_seq == 1
    n_heads = d // HEAD_DIM
    ng = len(GROUPS)
    mp = bp * seq
    ms = -(-bs // V7X_BF16_SUBLANES) * V7X_BF16_SUBLANES
    caches = (cache_kv_w128[0], cache_kv_w512[0], cache_kv_w2048[0])
    row = lambda v: v[:, None, :]
    b_in, conv_b, b_ra, b_ix, lam, b_out = (row(v) for v in (lru_b_in, lru_conv_b, lru_b_ra, lru_b_ix,
                                                               lru_lambda, lru_b_out))
    b1, b2 = row(mlp_b1), row(mlp_b2)
    g_ln, b_ln = ln_g[:, :, None, :], ln_b[:, :, None, :]

    xp = x_prompt.reshape(mp, d)
    xs = _pad_rows(x_sample.reshape(bs, d), ms)

    def norm(xf, y, i, k):
        return deepnorm(xf, y, g_ln[i][k], b_ln[i][k])

    def mlp(xfp, xbp, xfs, xbs, i, side_rolls=()):
        hp, hs, w2_bf = matmul(xbp, xbs, mlp_w1, i, b1[i], act="relu2", out_dtype=_BF16,
                               tiled_out=True, side_cast=(mlp_w2, i))
        yp, ys, *rolled = matmul_ktiled(hp, hs, w2_bf, b2[i], side_rolls=side_rolls)
        return norm(xfp, yp, i, 1), norm(xfs, ys, i, 1), rolled

    d_rnn = lru_conv_w.shape[2]
    up, us = matmul(xp.astype(_BF16), xs.astype(_BF16), lru_w_in, 0, b_in[0],
                    act="gelu", act_cols=d_rnn)
    lru_args = (lru_conv_w[0], conv_b[0], lru_w_ra[0], b_ra[0], lru_w_ix[0], b_ix[0], lam[0])
    up3 = up.reshape(bp, seq, 2 * d_rnn)
    hgp, h_p = rglru_prompt(up3, *lru_args)
    conv_p = up3[:, seq - (CONV_W - 1):, d_rnn:]
    conv_prev = [_pad_rows(state_conv[0][:, k], ms) for k in range(CONV_W - 1)]
    hgs, h_s = rglru_step(us, conv_prev, _pad_rows(state_h[0], ms), *lru_args)
    conv_s = jnp.concatenate([state_conv[0][:, 1:], us[:bs, None, d_rnn:]], axis=1)
    yp, ys = matmul(hgp.reshape(mp, d_rnn), hgs, lru_w_out, 0, b_out[0])
    (xfp, xbp), (xfs, xbs) = norm(xp, yp, 0, 0), norm(xs, ys, 0, 0)
    (xfp, xbp), (xfs, xbs), rolled_small = mlp(xfp, xbp, xfs, xbs, 0, side_rolls=caches[:-1])

    zero_bias = jnp.zeros((1, attn_w_qkv.shape[2]), _F32)
    qkv_p, qkv_s, rolled_big = matmul(xbp, xbs, attn_w_qkv, 0, zero_bias, slabs=True, side_rolls=caches[-1:])
    qkv_p = qkv_p.reshape(3 * ng * n_heads, bp, seq, HEAD_DIM)
    op = attn_prompt(qkv_p).reshape(mp, d)
    kv_p = []
    for g, (win, _) in enumerate(GROUPS):
        keep = min(win, seq)
        kv = qkv_p[(3 * g + 1) * n_heads:(3 * g + 3) * n_heads, :, seq - keep:]
        kv_p.append(jnp.transpose(kv, (1, 2, 0, 3)).reshape(1, bp, keep, 2, n_heads, HEAD_DIM))
    qkv_s = jnp.transpose(qkv_s[:, :bs], (1, 0, 2)).reshape(bs, 3 * ng, n_heads, HEAD_DIM)
    os_ = _pad_rows(attn_sample(qkv_s, caches, n_heads).reshape(bs, d), ms)
    yp, ys = matmul(op, os_, attn_w_o, 0, jnp.zeros((1, d), _F32))
    (xfp, xbp), (xfs, xbs) = norm(xfp, yp, 1, 0), norm(xfs, ys, 1, 0)
    (xfp, _), (xfs, _), _ = mlp(xfp, xbp, xfs, xbs, 1)

    rolled = (*rolled_small, rolled_big)
    kv_s = []
    for g in range(ng):
        new_row = qkv_s[:, 3 * g + 1:3 * g + 3].reshape(bs, 1, 2, n_heads, HEAD_DIM)
        kv_s.append(cache_set_last(rolled[g], new_row)[None])

    y_p = xfp.reshape(bp, seq, d)
    y_s = xfs[:bs].reshape(bs, 1, d)
    return (y_p, y_s, conv_p[None], h_p.reshape(1, bp, d_rnn), kv_p[0], kv_p[1], kv_p[2],
            conv_s[None], h_s[:bs][None], kv_s[0], kv_s[1], kv_s[2])
```

```python
import functools

import jax
import jax.numpy as jnp
from jax import lax
from jax.experimental import pallas as pl
from jax.experimental.pallas import tpu as pltpu

HEAD_DIM = 128
CONV_W = 4
LRU_C = 8.0
GROUPS = ((128, 1), (512, 4), (2048, 16))
KEYS_PER_BLOCK = 128
LN_EPS = 1e-5
DEPTH = 2
ALPHA = (2.0 * DEPTH) ** 0.25

V7X_SUBLANES = 8
V7X_BF16_SUBLANES = 16
V7X_VMEM_LIMIT_BYTES = 62 * 1024 * 1024

MM_TILE = 1024
MM_FULL_K = 4096
MM_KACC_TILE_M = 2048
MM_KACC_SUBTILES = 2
ATTN_UNROLL = 8
ATTN_MAX_STRIDE = 4
SIDE_ROLL_MIN_ROWS = 32
LN_ROWS = 256
SCAN_CHUNK = 256
SCAN_SUB = 64

_BF16 = jnp.bfloat16
_F32 = jnp.float32


def _params(*sem):
    return pltpu.CompilerParams(dimension_semantics=sem, vmem_limit_bytes=V7X_VMEM_LIMIT_BYTES)


def _apply_act(acc, act):
    if act == "gelu":
        return jax.nn.gelu(acc)
    if act == "relu2":
        r = jnp.maximum(acc, 0.0)
        return r * r
    return acc


def _roll_block(cur_ref, nxt_ref, out_ref):
    tl = out_ref.shape[0]
    out_ref[pl.ds(0, tl - 1)] = cur_ref[pl.ds(1, tl - 1)]
    out_ref[pl.ds(tl - 1, 1)] = nxt_ref[...]


def _mm_kernel(*refs, act, act_tiles, ck, nj, ni, side_blocks, cast_blocks, slabs, residual):
    ns = len(side_blocks)
    nc = 1 if cast_blocks else 0
    nr = 2 if residual else 0
    x_ref, wc_ref, b_ref, xs_ref = refs[:4]
    side_in = refs[4:4 + 2 * ns]
    cast_in = refs[4 + 2 * ns:4 + 2 * ns + nc]
    res_in = refs[4 + 2 * ns + nc:4 + 2 * ns + nc + nr]
    outs = refs[4 + 2 * ns + nc + nr:]
    o_ref, os_ref = outs[:2]
    side_out = outs[2:2 + ns]
    cast_out = outs[2 + ns:2 + ns + nc]
    wbf_ref = outs[2 + ns + nc]
    j = pl.program_id(0)
    i = pl.program_id(1)

    for c, n_side in enumerate(side_blocks):
        @pl.when(j * ni + i < n_side)
        def _(c=c):
            _roll_block(side_in[2 * c], side_in[2 * c + 1], side_out[c])

    if cast_blocks:
        @pl.when(j * ni + i < cast_blocks)
        def _():
            cast_out[0][...] = cast_in[0][...].astype(_BF16)

    @pl.when(j < nj)
    def _():
        r0 = pl.multiple_of(i * ck, ck)
        wbf_ref[j % 2, pl.ds(r0, ck), :] = wc_ref[...].astype(_BF16)

    def put(out_ref, val):
        if not slabs:
            out_ref[...] = val.astype(out_ref.dtype)
            return
        for h in range(out_ref.shape[0]):
            out_ref[h] = val[:, h * HEAD_DIM:(h + 1) * HEAD_DIM].astype(out_ref.dtype)

    def emit(lhs_ref, out_ref, res_ref=None):
        acc = jnp.dot(lhs_ref[...], wbf_ref[(j - 1) % 2], preferred_element_type=_F32) + b_ref[...]
        if res_ref is not None:
            acc = ALPHA * res_ref[...] + acc
        if act_tiles is None:
            put(out_ref, _apply_act(acc, act))
        else:
            @pl.when(j - 1 < act_tiles)
            def _():
                put(out_ref, _apply_act(acc, act))

            @pl.when(j - 1 >= act_tiles)
            def _():
                put(out_ref, acc)

    @pl.when(j > 0)
    def _():
        emit(x_ref, o_ref, res_in[0] if residual else None)

    @pl.when(jnp.logical_and(j > 0, i == 0))
    def _():
        emit(xs_ref, os_ref, res_in[1] if residual else None)


def _mm_acc_kernel(*refs, n_sub, ni, nk, side_blocks):
    ns = len(side_blocks)
    x_ref, w_ref, b_ref, xs_ref = refs[:4]
    side_in = refs[4:4 + 2 * ns]
    o_ref, os_ref = refs[4 + 2 * ns:6 + 2 * ns]
    side_out = refs[6 + 2 * ns:]
    i = pl.program_id(1)
    k = pl.program_id(2)

    for c, n_side in enumerate(side_blocks):
        @pl.when((pl.program_id(0) * ni + i) * nk + k < n_side)
        def _(c=c):
            _roll_block(side_in[2 * c], side_in[2 * c + 1], side_out[c])

    @pl.when(k == 0)
    def _():
        o_ref[...] = jnp.broadcast_to(b_ref[...], o_ref.shape)

    sub = w_ref.shape[0] // n_sub

    def contract(lhs_ref):
        return sum(jnp.dot(lhs_ref[s], w_ref[pl.ds(s * sub, sub), :], preferred_element_type=_F32)
                   for s in range(n_sub))

    o_ref[...] += contract(x_ref)

    @pl.when(jnp.logical_and(i == 0, k == 0))
    def _():
        os_ref[...] = jnp.broadcast_to(b_ref[...], os_ref.shape)

    @pl.when(i == 0)
    def _():
        os_ref[...] += contract(xs_ref)


def _roll_rows(cache, steps):
    bsz, length = cache.shape[:2]
    tl = min(SIDE_ROLL_MIN_ROWS, length)
    while bsz * (length // tl) > steps:
        assert tl < length
        tl *= 2
    return tl


def _side_roll_specs(cache, tl, step):
    bsz, length = cache.shape[:2]
    tail = cache.shape[2:]
    assert length % tl == 0
    per_batch = length // tl
    n_side = bsz * per_batch
    zeros = (0,) * len(tail)

    def where(*ids):
        t = jnp.minimum(step(*ids), n_side - 1)
        return t // per_batch, t % per_batch

    def cur_map(*ids):
        return where(*ids) + zeros

    def nxt_map(*ids):
        bi, ti = where(*ids)
        return (bi, jnp.minimum((ti + 1) * tl, length - 1)) + zeros

    cur = pl.BlockSpec((None, tl) + tail, cur_map)
    nxt = pl.BlockSpec((None, 1) + tail, nxt_map)
    return cur, nxt, n_side


def matmul(x, xs, w, layer, b, *, act=None, act_cols=None, out_dtype=_F32, side_rolls=(), side_cast=None,
           tiled_out=False, slabs=False, residual=None):
    m, kdim = x.shape
    ms = xs.shape[0]
    n = w.shape[2]
    tm = min(MM_TILE, m)
    tn = min(MM_TILE, n)
    assert kdim <= MM_FULL_K and m % tm == 0 and n % tn == 0
    ni, nj = m // tm, n // tn
    assert kdim % ni == 0
    ck = kdim // ni
    act_tiles = None
    if act_cols is not None:
        assert act_cols % tn == 0
        act_tiles = act_cols // tn
    side_in_specs, side_out_specs, side_blocks, side_args = [], [], [], []
    for cache in side_rolls:
        tl = _roll_rows(cache, (nj + 1) * ni)
        cur, nxt, n_side = _side_roll_specs(cache, tl, lambda j, i: j * ni + i)
        side_in_specs += [cur, nxt]
        side_out_specs.append(cur)
        side_blocks.append(n_side)
        side_args += [cache, cache]
    cast_blocks, cast_in_specs, cast_out_specs, cast_shapes = 0, [], [], []
    if side_cast is not None:
        w2, layer2 = side_cast
        _, k2, n2 = w2.shape
        cast_blocks = 1
        while cast_blocks * 2 <= (nj + 1) * ni and k2 % (cast_blocks * 2 * V7X_BF16_SUBLANES) == 0:
            cast_blocks *= 2
        rows2 = k2 // cast_blocks
        blk2 = lambda j, i: jnp.minimum(j * ni + i, cast_blocks - 1)
        cast_in_specs.append(pl.BlockSpec((None, rows2, n2), lambda j, i: (layer2, blk2(j, i), 0)))
        cast_out_specs.append(pl.BlockSpec((rows2, n2), lambda j, i: (blk2(j, i), 0)))
        cast_shapes.append(jax.ShapeDtypeStruct((k2, n2), _BF16))
        side_args.append(w2)
    res_specs = []
    if residual is not None:
        assert not tiled_out and not slabs
        res_specs = [pl.BlockSpec((tm, tn), lambda j, i: (jnp.where(j == 0, 0, i), jnp.maximum(j - 1, 0))),
                     pl.BlockSpec((ms, tn), lambda j, i: (0, jnp.maximum(j - 1, 0)))]
        side_args += list(residual)
    row_tile = lambda j, i: jnp.where(j == 0, 0, i)
    col_tile = lambda j, i: jnp.maximum(j - 1, 0)
    assert not (tiled_out and slabs)
    if tiled_out:
        shapes = [(nj, m, tn), (nj, ms, tn)]
        o_spec = pl.BlockSpec((None, tm, tn), lambda j, i: (col_tile(j, i), row_tile(j, i), 0))
        os_spec = pl.BlockSpec((None, ms, tn), lambda j, i: (col_tile(j, i), 0, 0))
    elif slabs:
        per_tile = tn // HEAD_DIM
        shapes = [(n // HEAD_DIM, m, HEAD_DIM), (n // HEAD_DIM, ms, HEAD_DIM)]
        o_spec = pl.BlockSpec((per_tile, tm, HEAD_DIM), lambda j, i: (col_tile(j, i), row_tile(j, i), 0))
        os_spec = pl.BlockSpec((per_tile, ms, HEAD_DIM), lambda j, i: (col_tile(j, i), 0, 0))
    else:
        shapes = [(m, n), (ms, n)]
        o_spec = pl.BlockSpec((tm, tn), lambda j, i: (row_tile(j, i), col_tile(j, i)))
        os_spec = pl.BlockSpec((ms, tn), lambda j, i: (0, col_tile(j, i)))
    return pl.pallas_call(
        functools.partial(_mm_kernel, act=act, act_tiles=act_tiles, ck=ck, nj=nj, ni=ni,
                          side_blocks=tuple(side_blocks), cast_blocks=cast_blocks, slabs=slabs,
                          residual=residual is not None),
        out_shape=(*[jax.ShapeDtypeStruct(s, out_dtype) for s in shapes],
                   *[jax.ShapeDtypeStruct(c.shape, c.dtype) for c in side_rolls], *cast_shapes),
        grid=(nj + 1, ni),
        in_specs=[
            pl.BlockSpec((tm, kdim), lambda j, i: (row_tile(j, i), 0)),
            pl.BlockSpec((None, ck, tn),
                         lambda j, i: (layer, jnp.where(j < nj, i, ni - 1), jnp.minimum(j, nj - 1))),
            pl.BlockSpec((1, tn), lambda j, i: (0, col_tile(j, i))),
            pl.BlockSpec((ms, kdim), lambda j, i: (0, 0)),
            *side_in_specs, *cast_in_specs, *res_specs,
        ],
        out_specs=(o_spec, os_spec, *side_out_specs, *cast_out_specs),
        scratch_shapes=[pltpu.VMEM((2, kdim, tn), _BF16)],
        compiler_params=_params("arbitrary", "arbitrary"),
        name="matmul",
    )(x, w, b, xs, *side_args)


def matmul_ktiled(x, xs, w, b, side_rolls=()):
    nkt, m, tk = x.shape
    ms = xs.shape[1]
    n = w.shape[1]
    tm = min(MM_KACC_TILE_M, m)
    tn = min(MM_TILE, n)
    n_sub = min(MM_KACC_SUBTILES, nkt)
    assert m % tm == 0 and n % tn == 0 and nkt % n_sub == 0 and w.shape[0] == nkt * tk
    nj, ni, nk = n // tn, m // tm, nkt // n_sub
    side_in_specs, side_out_specs, side_blocks, side_args = [], [], [], []
    for cache in side_rolls:
        tl = _roll_rows(cache, nj * ni * nk)
        cur, nxt, n_side = _side_roll_specs(cache, tl, lambda j, i, k: (j * ni + i) * nk + k)
        side_in_specs += [cur, nxt]
        side_out_specs.append(cur)
        side_blocks.append(n_side)
        side_args += [cache, cache]
    return pl.pallas_call(
        functools.partial(_mm_acc_kernel, n_sub=n_sub, ni=ni, nk=nk, side_blocks=tuple(side_blocks)),
        out_shape=(jax.ShapeDtypeStruct((m, n), _F32), jax.ShapeDtypeStruct((ms, n), _F32),
                   *[jax.ShapeDtypeStruct(c.shape, c.dtype) for c in side_rolls]),
        grid=(nj, ni, nk),
        in_specs=[
            pl.BlockSpec((n_sub, tm, tk), lambda j, i, k: (k, i, 0)),
            pl.BlockSpec((n_sub * tk, tn), lambda j, i, k: (k, j)),
            pl.BlockSpec((1, tn), lambda j, i, k: (0, j)),
            pl.BlockSpec((n_sub, ms, tk), lambda j, i, k: (k, 0, 0)),
            *side_in_specs,
        ],
        out_specs=(pl.BlockSpec((tm, tn), lambda j, i, k: (i, j)),
                   pl.BlockSpec((ms, tn), lambda j, i, k: (0, j)), *side_out_specs),
        compiler_params=_params("arbitrary", "arbitrary", "arbitrary"),
        name="matmul_ktiled",
    )(x, w, b, xs, *side_args)


def _ln_kernel(*refs, fused_residual):
    if fused_residual:
        z_ref, g_ref, b_ref, o_ref, obf_ref = refs
        z = z_ref[...]
    else:
        x_ref, y_ref, g_ref, b_ref, o_ref, obf_ref = refs
        z = ALPHA * x_ref[...] + y_ref[...]
    mu = jnp.mean(z, axis=-1, keepdims=True)
    zc = z - mu
    var = jnp.mean(zc * zc, axis=-1, keepdims=True)
    out = zc * lax.rsqrt(var + LN_EPS) * g_ref[...] + b_ref[...]
    o_ref[...] = out
    obf_ref[...] = out.astype(_BF16)


def deepnorm(x, y, g, b):
    m, d = y.shape
    tm = min(LN_ROWS, m)
    assert m % tm == 0
    row = pl.BlockSpec((tm, d), lambda i: (i, 0))
    vec = pl.BlockSpec((1, d), lambda i: (0, 0))
    rows_in = [y] if x is None else [x, y]
    return pl.pallas_call(
        functools.partial(_ln_kernel, fused_residual=x is None),
        out_shape=(jax.ShapeDtypeStruct((m, d), _F32), jax.ShapeDtypeStruct((m, d), _BF16)),
        grid=(m // tm,),
        in_specs=[row] * len(rows_in) + [vec, vec],
        out_specs=(row, row),
        compiler_params=_params("arbitrary"),
        name="deepnorm",
    )(*rows_in, g, b)


def _softplus(x):
    return jnp.maximum(x, 0.0) + jnp.log1p(jnp.exp(-jnp.abs(x)))


def _lru_coeffs(xc, wra, bra, wix, bix, sp):
    xcb = xc.astype(_BF16)
    r = jax.nn.sigmoid(jnp.dot(xcb, wra, preferred_element_type=_F32) + bra)
    i = jax.nn.sigmoid(jnp.dot(xcb, wix, preferred_element_type=_F32) + bix)
    log_a = -LRU_C * r * sp
    a = jnp.exp(log_a)
    t = jnp.tanh(log_a)
    bt = jnp.sqrt(-2.0 * t / (1.0 - t)) * (i * xc)
    return a, bt


def _shift_rows(x, s, fill):
    rows = lax.broadcasted_iota(jnp.int32, x.shape, 0)
    return jnp.where(rows < s, fill, pltpu.roll(x, s, 0))


def _rglru_kernel(xb_ref, gate_ref, cw_ref, cb_ref, wra_ref, bra_ref, wix_ref, bix_ref, lam_ref,
                  o_ref, hlast_ref, xpad_ref, *, seq, chunk):
    pad = V7X_SUBLANES
    cw = cw_ref[...]
    cbias = cb_ref[...]
    wra = wra_ref[...].astype(_BF16)
    wix = wix_ref[...].astype(_BF16)
    bra = bra_ref[...]
    bix = bix_ref[...]
    sp = _softplus(-lam_ref[...])

    xpad_ref[pl.ds(0, pad), :] = jnp.zeros((pad, xpad_ref.shape[1]), _F32)
    xpad_ref[pl.ds(pad, seq), :] = xb_ref[...]

    def body(c, h):
        t0 = pl.multiple_of(c * chunk, chunk)
        xw = xpad_ref[pl.ds(t0, chunk + pad), :]
        xc = cbias
        for k in range(CONV_W):
            off = pad - (CONV_W - 1) + k
            xc = xc + cw[k:k + 1, :] * xw[off:off + chunk, :]
        a, bt = _lru_coeffs(xc, wra, bra, wix, bix, sp)
        hs = []
        for q in range(chunk // SCAN_SUB):
            aq = a[q * SCAN_SUB:(q + 1) * SCAN_SUB]
            bq = bt[q * SCAN_SUB:(q + 1) * SCAN_SUB]
            s = 1
            while s < SCAN_SUB:
                if s < V7X_SUBLANES:
                    a_sh = _shift_rows(aq, s, 1.0)
                    b_sh = _shift_rows(bq, s, 0.0)
                    bq = aq * b_sh + bq
                    aq = aq * a_sh
                else:
                    bq = jnp.concatenate([bq[:s], aq[s:] * bq[:-s] + bq[s:]], axis=0)
                    aq = jnp.concatenate([aq[:s], aq[s:] * aq[:-s]], axis=0)
                s *= 2
            hq = aq * h + bq
            h = hq[SCAN_SUB - 1:SCAN_SUB, :]
            hs.append(hq)
        hs = jnp.concatenate(hs, axis=0)
        o_ref[pl.ds(t0, chunk), :] = (hs * gate_ref[pl.ds(t0, chunk), :]).astype(o_ref.dtype)
        return h

    h0 = jnp.zeros((1, xb_ref.shape[1]), _F32)
    hlast_ref[...] = lax.fori_loop(0, seq // chunk, body, h0)


def rglru_prompt(u, cw, cb, wra, bra, wix, bix, lam):
    bsz, seq, c2 = u.shape
    c = c2 // 2
    nblk, blk, _ = wra.shape
    assert c == nblk * blk and seq % SCAN_CHUNK == 0
    col = lambda b, n: (b, 0, n)
    vec = pl.BlockSpec((1, blk), lambda b, n: (0, n))
    wspec = pl.BlockSpec((None, blk, blk), lambda b, n: (n, 0, 0))
    return pl.pallas_call(
        functools.partial(_rglru_kernel, seq=seq, chunk=SCAN_CHUNK),
        out_shape=(jax.ShapeDtypeStruct((bsz, seq, c), _BF16), jax.ShapeDtypeStruct((bsz, 1, c), _F32)),
        grid=(bsz, nblk),
        in_specs=[
            pl.BlockSpec((None, seq, blk), lambda b, n: (b, 0, n + nblk)),
            pl.BlockSpec((None, seq, blk), col),
            pl.BlockSpec((CONV_W, blk), lambda b, n: (0, n)),
            vec, wspec, vec, wspec, vec, vec,
        ],
        out_specs=(pl.BlockSpec((None, seq, blk), col), pl.BlockSpec((None, 1, blk), col)),
        scratch_shapes=[pltpu.VMEM((seq + V7X_SUBLANES, blk), _F32)],
        compiler_params=_params("arbitrary", "arbitrary"),
        name="rglru_prompt",
    )(u, u, cw, cb, wra, bra, wix, bix, lam)


def _rglru_step_kernel(xb_ref, gate_ref, c0_ref, c1_ref, c2_ref, h_ref, cw_ref, cb_ref,
                       wra_ref, bra_ref, wix_ref, bix_ref, lam_ref, o_ref, hnew_ref):
    cw = cw_ref[...]
    xc = (cb_ref[...] + cw[0:1, :] * c0_ref[...] + cw[1:2, :] * c1_ref[...]
          + cw[2:3, :] * c2_ref[...] + cw[3:4, :] * xb_ref[...])
    a, bt = _lru_coeffs(xc, wra_ref[...].astype(_BF16), bra_ref[...], wix_ref[...].astype(_BF16),
                        bix_ref[...], _softplus(-lam_ref[...]))
    h = a * h_ref[...] + bt
    hnew_ref[...] = h
    o_ref[...] = (h * gate_ref[...]).astype(o_ref.dtype)


def rglru_step(u, conv_prev, h_prev, cw, cb, wra, bra, wix, bix, lam):
    rows, c2 = u.shape
    c = c2 // 2
    nblk, blk, _ = wra.shape
    col = pl.BlockSpec((rows, blk), lambda n: (0, n))
    vec = pl.BlockSpec((1, blk), lambda n: (0, n))
    wspec = pl.BlockSpec((None, blk, blk), lambda n: (n, 0, 0))
    return pl.pallas_call(
        _rglru_step_kernel,
        out_shape=(jax.ShapeDtypeStruct((rows, c), _BF16), jax.ShapeDtypeStruct((rows, c), _F32)),
        grid=(nblk,),
        in_specs=[pl.BlockSpec((rows, blk), lambda n: (0, n + nblk)), col, col, col, col, col,
                  pl.BlockSpec((CONV_W, blk), lambda n: (0, n)), vec, wspec, vec, wspec, vec, vec],
        out_specs=(col, col),
        compiler_params=_params("arbitrary"),
        name="rglru_step",
    )(u, u, *conv_prev, h_prev, cw, cb, wra, bra, wix, bix, lam)


def _attn_prompt_kernel(*refs, seq):
    qkv_refs = refs[:9]
    o_ref = refs[9]
    acc_ref, m_ref, l_ref, stage_in_ref, stage_out_ref = refs[10:15]
    kb = KEYS_PER_BLOCK
    scale = HEAD_DIM ** -0.5
    qi = lax.broadcasted_iota(jnp.int32, (kb, kb), 0)
    ki = lax.broadcasted_iota(jnp.int32, (kb, kb), 1)
    cur_mask = ki <= qi
    qi2 = lax.broadcasted_iota(jnp.int32, (kb, 2 * kb), 0)
    ki2 = lax.broadcasted_iota(jnp.int32, (kb, 2 * kb), 1)
    cur_part = jnp.logical_and(ki2 >= kb, ki2 - kb <= qi2)
    prev_part = jnp.logical_and(ki2 < kb, ki2 >= qi2)
    n_blocks = seq // kb
    assert n_blocks % ATTN_UNROLL == 0

    def strided(start, n, st):
        return pl.ds(start, n) if st == 1 else pl.ds(start, n, stride=st)

    def run_group(load, store, st, seg_len):
        per_res = seg_len // st
        nblk = per_res // kb
        assert per_res % kb == 0 and n_blocks % (st * nblk) == 0

        def block_rows(idx):
            seg = idx // (st * nblk)
            r = (idx // nblk) % st
            j = idx % nblk
            first = seg * seg_len + r
            start = first + j * (kb * st)
            prows = strided(jnp.maximum(start - kb * st, first), kb, st)
            return strided(start, kb, st), prows, j

        def scores(rows, prows, j):
            q = load(0, rows).astype(_BF16)
            k = load(1, rows).astype(_BF16)
            if nblk > 1:
                k = jnp.concatenate([load(1, prows).astype(_BF16), k], axis=0)
                mask = jnp.logical_or(cur_part, jnp.logical_and(prev_part, j > 0))
            else:
                mask = cur_mask
            s = lax.dot_general(q, k, (((1,), (1,)), ((), ())), preferred_element_type=_F32) * scale
            return jnp.where(mask, s, -jnp.inf)

        def values(rows, prows):
            v = load(2, rows).astype(_BF16)
            if nblk > 1:
                v = jnp.concatenate([load(2, prows).astype(_BF16), v], axis=0)
            return v

        def blocks(it, carry):
            where = [block_rows(it * ATTN_UNROLL + u) for u in range(ATTN_UNROLL)]
            ss = [scores(rows, prows, j) for rows, prows, j in where]
            ms = [jnp.max(s, axis=-1, keepdims=True) for s in ss]
            ps = [jnp.exp(s - m) for s, m in zip(ss, ms)]
            for (rows, prows, _), m, p in zip(where, ms, ps):
                store(0, rows, jnp.dot(p.astype(_BF16), values(rows, prows), preferred_element_type=_F32))
                store(1, rows, jnp.broadcast_to(m, (kb, HEAD_DIM)))
                store(2, rows, jnp.broadcast_to(jnp.sum(p, axis=-1, keepdims=True), (kb, HEAD_DIM)))
            return carry

        lax.fori_loop(0, n_blocks // ATTN_UNROLL, blocks, 0)

    stats = (acc_ref, m_ref, l_ref)
    for g, (_, dil) in enumerate(GROUPS):
        src = qkv_refs[3 * g:3 * g + 3]

        def store_tok(a, rows, val, g=g):
            stats[a][g, rows, :] = val

        if dil <= ATTN_MAX_STRIDE:
            run_group(lambda a, rows, src=src: src[a][rows, :], store_tok, dil, seq)
            continue
        inner = ATTN_MAX_STRIDE
        outer = dil // inner
        assert dil == inner * outer
        seg_len = seq // outer
        for a in range(3):
            for r1 in range(outer):
                for c in range(seg_len // kb):
                    stage_in_ref[a, pl.ds(r1 * seg_len + c * kb, kb), :] = (
                        src[a][strided(r1 + c * kb * outer, kb, outer), :])

        def store_seg(a, rows, val):
            stage_out_ref[a, rows, :] = val

        run_group(lambda a, rows: stage_in_ref[a, rows, :], store_seg, inner, seg_len)
        for a in range(3):
            for r1 in range(outer):
                for c in range(seg_len // kb):
                    stats[a][g, strided(r1 + c * kb * outer, kb, outer), :] = (
                        stage_out_ref[a, pl.ds(r1 * seg_len + c * kb, kb), :])

    def merge(c, carry):
        t0 = pl.multiple_of(c * kb, kb)
        ms = [m_ref[g, pl.ds(t0, kb), :] for g in range(len(GROUPS))]
        mx = functools.reduce(jnp.maximum, ms)
        num = jnp.zeros((kb, HEAD_DIM), _F32)
        den = jnp.zeros((kb, HEAD_DIM), _F32)
        for g in range(len(GROUPS)):
            w = jnp.exp(ms[g] - mx)
            num = num + w * acc_ref[g, pl.ds(t0, kb), :]
            den = den + w * l_ref[g, pl.ds(t0, kb), :]
        o_ref[pl.ds(t0, kb), :] = (num / den).astype(o_ref.dtype)
        return carry

    lax.fori_loop(0, seq // kb, merge, 0)


def attn_prompt(qkv):
    n_slabs, bsz, seq, _ = qkv.shape
    ng = len(GROUPS)
    n_heads = n_slabs // (3 * ng)

    def spec(section):
        return pl.BlockSpec((None, None, seq, HEAD_DIM), lambda b, h: (section * n_heads + h, b, 0, 0))

    return pl.pallas_call(
        functools.partial(_attn_prompt_kernel, seq=seq),
        out_shape=jax.ShapeDtypeStruct((bsz, seq, n_heads * HEAD_DIM), _BF16),
        grid=(bsz, n_heads),
        in_specs=[spec(s) for s in range(3 * ng)],
        out_specs=pl.BlockSpec((None, seq, HEAD_DIM), lambda b, h: (b, 0, h)),
        scratch_shapes=[pltpu.VMEM((ng, seq, HEAD_DIM), _F32)] * 3 + [pltpu.VMEM((3, seq, HEAD_DIM), _F32)] * 2,
        compiler_params=_params("arbitrary", "arbitrary"),
        name="attn_prompt",
    )(*([qkv] * (3 * ng)))


def _attn_sample_kernel(*refs):
    ng = len(GROUPS)
    q_refs = refs[0:ng]
    knew_refs = refs[ng:2 * ng]
    vnew_refs = refs[2 * ng:3 * ng]
    kc_refs = refs[3 * ng:4 * ng]
    vc_refs = refs[4 * ng:5 * ng]
    o_ref = refs[5 * ng]
    scale = HEAD_DIM ** -0.5
    parts = []
    for g in range(ng):
        q = q_refs[g][...]
        s_new = jnp.sum(q * knew_refs[g][...], axis=-1, keepdims=True) * scale
        s_old = jnp.sum(q[None] * kc_refs[g][...], axis=-1, keepdims=True) * scale
        m = jnp.maximum(jnp.max(s_old, axis=0), s_new)
        p_new = jnp.exp(s_new - m)
        p_old = jnp.exp(s_old - m[None])
        l = p_new + jnp.sum(p_old, axis=0)
        acc = p_new * vnew_refs[g][...] + jnp.sum(p_old * vc_refs[g][...], axis=0)
        parts.append((m, l, acc))
    mx = functools.reduce(jnp.maximum, [p[0] for p in parts])
    num = sum(jnp.exp(m - mx) * acc for m, _, acc in parts)
    den = sum(jnp.exp(m - mx) * l for m, l, _ in parts)
    o_ref[...] = (num / den).astype(o_ref.dtype)


def attn_sample(qkv, caches, n_heads):
    bsz = qkv.shape[0]
    ng = len(GROUPS)
    kb = KEYS_PER_BLOCK

    def new_spec(section):
        return pl.BlockSpec((None, None, n_heads, HEAD_DIM), lambda b: (b, section, 0, 0))

    def cache_spec(kv):
        return pl.BlockSpec((None, kb, None, None, n_heads, HEAD_DIM), lambda b: (b, 0, 0, kv, 0, 0))

    strided = []
    for (win, dil), cache in zip(GROUPS, caches):
        assert cache.shape[1] == win == kb * dil
        strided.append(cache.reshape(bsz, kb, dil, 2, n_heads, HEAD_DIM))
    in_specs = ([new_spec(3 * g) for g in range(ng)] + [new_spec(3 * g + 1) for g in range(ng)]
                + [new_spec(3 * g + 2) for g in range(ng)] + [cache_spec(0)] * ng + [cache_spec(1)] * ng)
    return pl.pallas_call(
        _attn_sample_kernel,
        out_shape=jax.ShapeDtypeStruct((bsz, n_heads, HEAD_DIM), _BF16),
        grid=(bsz,),
        in_specs=in_specs,
        out_specs=pl.BlockSpec((None, n_heads, HEAD_DIM), lambda b: (b, 0, 0)),
        compiler_params=_params("arbitrary"),
        name="attn_sample",
    )(*([qkv] * (3 * ng)), *strided, *strided)


def _set_last_kernel(rolled_ref, new_ref, o_ref):
    del rolled_ref
    o_ref[...] = new_ref[...]


def cache_set_last(rolled, new_row):
    bsz, length = rolled.shape[:2]
    tail = rolled.shape[2:]
    zeros = (0,) * len(tail)
    return pl.pallas_call(
        _set_last_kernel,
        out_shape=jax.ShapeDtypeStruct(rolled.shape, rolled.dtype),
        grid=(bsz,),
        in_specs=[pl.BlockSpec(memory_space=pl.ANY),
                  pl.BlockSpec((None, 1) + tail, lambda b: (b, 0) + zeros)],
        out_specs=pl.BlockSpec((None, 1) + tail, lambda b: (b, length - 1) + zeros),
        input_output_aliases={0: 0},
        compiler_params=_params("arbitrary"),
        name="cache_set_last",
    )(rolled, new_row)


def _pad_rows(a, rows):
    return jnp.pad(a, ((0, rows - a.shape[0]),) + ((0, 0),) * (a.ndim - 1))


def kernel(x_prompt, x_sample, state_conv, state_h, cache_kv_w128, cache_kv_w512, cache_kv_w2048,
           lru_w_in, lru_b_in, lru_conv_w, lru_conv_b, lru_w_ra, lru_b_ra, lru_w_ix, lru_b_ix,
           lru_lambda, lru_w_out, lru_b_out, attn_w_qkv, attn_w_o,
           mlp_w1, mlp_b1, mlp_w2, mlp_b2, ln_g, ln_b):
    bp, seq, d = x_prompt.shape
    bs, dec_seq, _ = x_sample.shape
    assert dec_seq == 1
    n_heads = d // HEAD_DIM
    ng = len(GROUPS)
    mp = bp * seq
    ms = -(-bs // V7X_BF16_SUBLANES) * V7X_BF16_SUBLANES
    caches = (cache_kv_w128[0], cache_kv_w512[0], cache_kv_w2048[0])
    row = lambda v: v[:, None, :]
    b_in, conv_b, b_ra, b_ix, lam, b_out = (row(v) for v in (lru_b_in, lru_conv_b, lru_b_ra, lru_b_ix,
                                                               lru_lambda, lru_b_out))
    b1, b2 = row(mlp_b1), row(mlp_b2)
    g_ln, b_ln = ln_g[:, :, None, :], ln_b[:, :, None, :]

    xp = x_prompt.reshape(mp, d)
    xs = _pad_rows(x_sample.reshape(bs, d), ms)

    def norm(xf, y, i, k):
        return deepnorm(xf, y, g_ln[i][k], b_ln[i][k])

    def mlp(xfp, xbp, xfs, xbs, i, side_rolls=()):
        hp, hs, w2_bf = matmul(xbp, xbs, mlp_w1, i, b1[i], act="relu2", out_dtype=_BF16,
                               tiled_out=True, side_cast=(mlp_w2, i))
        yp, ys, *rolled = matmul_ktiled(hp, hs, w2_bf, b2[i], side_rolls=side_rolls)
        return norm(xfp, yp, i, 1), norm(xfs, ys, i, 1), rolled

    d_rnn = lru_conv_w.shape[2]
    up, us = matmul(xp.astype(_BF16), xs.astype(_BF16), lru_w_in, 0, b_in[0],
                    act="gelu", act_cols=d_rnn)
    lru_args = (lru_conv_w[0], conv_b[0], lru_w_ra[0], b_ra[0], lru_w_ix[0], b_ix[0], lam[0])
    up3 = up.reshape(bp, seq, 2 * d_rnn)
    hgp, h_p = rglru_prompt(up3, *lru_args)
    conv_p = up3[:, seq - (CONV_W - 1):, d_rnn:]
    conv_prev = [_pad_rows(state_conv[0][:, k], ms) for k in range(CONV_W - 1)]
    hgs, h_s = rglru_step(us, conv_prev, _pad_rows(state_h[0], ms), *lru_args)
    conv_s = jnp.concatenate([state_conv[0][:, 1:], us[:bs, None, d_rnn:]], axis=1)
    zp, zs = matmul(hgp.reshape(mp, d_rnn), hgs, lru_w_out, 0, b_out[0], residual=(xp, xs))
    (xfp, xbp), (xfs, xbs) = norm(None, zp, 0, 0), norm(None, zs, 0, 0)
    (xfp, xbp), (xfs, xbs), rolled_small = mlp(xfp, xbp, xfs, xbs, 0, side_rolls=caches[:-1])

    zero_bias = jnp.zeros((1, attn_w_qkv.shape[2]), _F32)
    qkv_p, qkv_s, rolled_big = matmul(xbp, xbs, attn_w_qkv, 0, zero_bias, slabs=True, side_rolls=caches[-1:])
    qkv_p = qkv_p.reshape(3 * ng * n_heads, bp, seq, HEAD_DIM)
    op = attn_prompt(qkv_p).reshape(mp, d)
    kv_p = []
    for g, (win, _) in enumerate(GROUPS):
        keep = min(win, seq)
        kv = qkv_p[(3 * g + 1) * n_heads:(3 * g + 3) * n_heads, :, seq - keep:]
        kv_p.append(jnp.transpose(kv, (1, 2, 0, 3)).reshape(1, bp, keep, 2, n_heads, HEAD_DIM))
    qkv_s = jnp.transpose(qkv_s[:, :bs], (1, 0, 2)).reshape(bs, 3 * ng, n_heads, HEAD_DIM)
    os_ = _pad_rows(attn_sample(qkv_s, caches, n_heads).reshape(bs, d), ms)
    zp, zs = matmul(op, os_, attn_w_o, 0, jnp.zeros((1, d), _F32), residual=(xfp, xfs))
    (xfp, xbp), (xfs, xbs) = norm(None, zp, 1, 0), norm(None, zs, 1, 0)
    (xfp, _), (xfs, _), _ = mlp(xfp, xbp, xfs, xbs, 1)

    rolled = (*rolled_small, rolled_big)
    kv_s = []
    for g in range(ng):
        new_row = qkv_s[:, 3 * g + 1:3 * g + 3].reshape(bs, 1, 2, n_heads, HEAD_DIM)
        kv_s.append(cache_set_last(rolled[g], new_row)[None])

    y_p = xfp.reshape(bp, seq, d)
    y_s = xfs[:bs].reshape(bs, 1, d)
    return (y_p, y_s, conv_p[None], h_p.reshape(1, bp, d_rnn), kv_p[0], kv_p[1], kv_p[2],
            conv_s[None], h_s[:bs][None], kv_s[0], kv_s[1], kv_s[2])
```

```python
import functools

import jax
import jax.numpy as jnp
from jax import lax
from jax.experimental import pallas as pl
from jax.experimental.pallas import tpu as pltpu

HEAD_DIM = 128
CONV_W = 4
LRU_C = 8.0
GROUPS = ((128, 1), (512, 4), (2048, 16))
KEYS_PER_BLOCK = 128
LN_EPS = 1e-5
DEPTH = 2
ALPHA = (2.0 * DEPTH) ** 0.25

V7X_SUBLANES = 8
V7X_BF16_SUBLANES = 16
V7X_VMEM_LIMIT_BYTES = 62 * 1024 * 1024

MM_TILE = 1024
MM_FULL_K = 4096
MM_KACC_TILE_M = 2048
MM_KACC_SUBTILES = 2
ATTN_UNROLL = 16
ATTN_MAX_STRIDE = 4
SIDE_ROLL_MIN_ROWS = 32
LN_ROWS = 256
SCAN_CHUNK = 256
SCAN_SUB = 64

_BF16 = jnp.bfloat16
_F32 = jnp.float32


def _params(*sem):
    return pltpu.CompilerParams(dimension_semantics=sem, vmem_limit_bytes=V7X_VMEM_LIMIT_BYTES)


def _apply_act(acc, act):
    if act == "gelu":
        return jax.nn.gelu(acc)
    if act == "relu2":
        r = jnp.maximum(acc, 0.0)
        return r * r
    return acc


def _roll_block(cur_ref, nxt_ref, out_ref):
    tl = out_ref.shape[0]
    out_ref[pl.ds(0, tl - 1)] = cur_ref[pl.ds(1, tl - 1)]
    out_ref[pl.ds(tl - 1, 1)] = nxt_ref[...]


def _mm_kernel(*refs, act, act_tiles, ck, nj, ni, side_blocks, cast_blocks, slabs, residual):
    ns = len(side_blocks)
    nc = 1 if cast_blocks else 0
    nr = 2 if residual else 0
    x_ref, wc_ref, b_ref, xs_ref = refs[:4]
    side_in = refs[4:4 + 2 * ns]
    cast_in = refs[4 + 2 * ns:4 + 2 * ns + nc]
    res_in = refs[4 + 2 * ns + nc:4 + 2 * ns + nc + nr]
    outs = refs[4 + 2 * ns + nc + nr:]
    o_ref, os_ref = outs[:2]
    side_out = outs[2:2 + ns]
    cast_out = outs[2 + ns:2 + ns + nc]
    wbf_ref = outs[2 + ns + nc]
    j = pl.program_id(0)
    i = pl.program_id(1)

    for c, n_side in enumerate(side_blocks):
        @pl.when(j * ni + i < n_side)
        def _(c=c):
            _roll_block(side_in[2 * c], side_in[2 * c + 1], side_out[c])

    if cast_blocks:
        @pl.when(j * ni + i < cast_blocks)
        def _():
            cast_out[0][...] = cast_in[0][...].astype(_BF16)

    @pl.when(j < nj)
    def _():
        r0 = pl.multiple_of(i * ck, ck)
        wbf_ref[j % 2, pl.ds(r0, ck), :] = wc_ref[...].astype(_BF16)

    def put(out_ref, val):
        if not slabs:
            out_ref[...] = val.astype(out_ref.dtype)
            return
        for h in range(out_ref.shape[0]):
            out_ref[h] = val[:, h * HEAD_DIM:(h + 1) * HEAD_DIM].astype(out_ref.dtype)

    def emit(lhs_ref, out_ref, res_ref=None):
        acc = jnp.dot(lhs_ref[...], wbf_ref[(j - 1) % 2], preferred_element_type=_F32) + b_ref[...]
        if res_ref is not None:
            acc = ALPHA * res_ref[...] + acc
        if act_tiles is None:
            put(out_ref, _apply_act(acc, act))
        else:
            @pl.when(j - 1 < act_tiles)
            def _():
                put(out_ref, _apply_act(acc, act))

            @pl.when(j - 1 >= act_tiles)
            def _():
                put(out_ref, acc)

    @pl.when(j > 0)
    def _():
        emit(x_ref, o_ref, res_in[0] if residual else None)

    @pl.when(jnp.logical_and(j > 0, i == 0))
    def _():
        emit(xs_ref, os_ref, res_in[1] if residual else None)


def _mm_acc_kernel(*refs, n_sub, ni, nk, side_blocks):
    ns = len(side_blocks)
    x_ref, w_ref, b_ref, xs_ref = refs[:4]
    side_in = refs[4:4 + 2 * ns]
    o_ref, os_ref = refs[4 + 2 * ns:6 + 2 * ns]
    side_out = refs[6 + 2 * ns:]
    i = pl.program_id(1)
    k = pl.program_id(2)

    for c, n_side in enumerate(side_blocks):
        @pl.when((pl.program_id(0) * ni + i) * nk + k < n_side)
        def _(c=c):
            _roll_block(side_in[2 * c], side_in[2 * c + 1], side_out[c])

    @pl.when(k == 0)
    def _():
        o_ref[...] = jnp.broadcast_to(b_ref[...], o_ref.shape)

    sub = w_ref.shape[0] // n_sub

    def contract(lhs_ref):
        return sum(jnp.dot(lhs_ref[s], w_ref[pl.ds(s * sub, sub), :], preferred_element_type=_F32)
                   for s in range(n_sub))

    o_ref[...] += contract(x_ref)

    @pl.when(jnp.logical_and(i == 0, k == 0))
    def _():
        os_ref[...] = jnp.broadcast_to(b_ref[...], os_ref.shape)

    @pl.when(i == 0)
    def _():
        os_ref[...] += contract(xs_ref)


def _roll_rows(cache, steps):
    bsz, length = cache.shape[:2]
    tl = min(SIDE_ROLL_MIN_ROWS, length)
    while bsz * (length // tl) > steps:
        assert tl < length
        tl *= 2
    return tl


def _side_roll_specs(cache, tl, step):
    bsz, length = cache.shape[:2]
    tail = cache.shape[2:]
    assert length % tl == 0
    per_batch = length // tl
    n_side = bsz * per_batch
    zeros = (0,) * len(tail)

    def where(*ids):
        t = jnp.minimum(step(*ids), n_side - 1)
        return t // per_batch, t % per_batch

    def cur_map(*ids):
        return where(*ids) + zeros

    def nxt_map(*ids):
        bi, ti = where(*ids)
        return (bi, jnp.minimum((ti + 1) * tl, length - 1)) + zeros

    cur = pl.BlockSpec((None, tl) + tail, cur_map)
    nxt = pl.BlockSpec((None, 1) + tail, nxt_map)
    return cur, nxt, n_side


def matmul(x, xs, w, layer, b, *, act=None, act_cols=None, out_dtype=_F32, side_rolls=(), side_cast=None,
           tiled_out=False, slabs=False, residual=None):
    m, kdim = x.shape
    ms = xs.shape[0]
    n = w.shape[2]
    tm = min(MM_TILE, m)
    tn = min(MM_TILE, n)
    assert kdim <= MM_FULL_K and m % tm == 0 and n % tn == 0
    ni, nj = m // tm, n // tn
    assert kdim % ni == 0
    ck = kdim // ni
    act_tiles = None
    if act_cols is not None:
        assert act_cols % tn == 0
        act_tiles = act_cols // tn
    side_in_specs, side_out_specs, side_blocks, side_args = [], [], [], []
    for cache in side_rolls:
        tl = _roll_rows(cache, (nj + 1) * ni)
        cur, nxt, n_side = _side_roll_specs(cache, tl, lambda j, i: j * ni + i)
        side_in_specs += [cur, nxt]
        side_out_specs.append(cur)
        side_blocks.append(n_side)
        side_args += [cache, cache]
    cast_blocks, cast_in_specs, cast_out_specs, cast_shapes = 0, [], [], []
    if side_cast is not None:
        w2, layer2 = side_cast
        _, k2, n2 = w2.shape
        cast_blocks = 1
        while cast_blocks * 2 <= (nj + 1) * ni and k2 % (cast_blocks * 2 * V7X_BF16_SUBLANES) == 0:
            cast_blocks *= 2
        rows2 = k2 // cast_blocks
        blk2 = lambda j, i: jnp.minimum(j * ni + i, cast_blocks - 1)
        cast_in_specs.append(pl.BlockSpec((None, rows2, n2), lambda j, i: (layer2, blk2(j, i), 0)))
        cast_out_specs.append(pl.BlockSpec((rows2, n2), lambda j, i: (blk2(j, i), 0)))
        cast_shapes.append(jax.ShapeDtypeStruct((k2, n2), _BF16))
        side_args.append(w2)
    res_specs = []
    if residual is not None:
        assert not tiled_out and not slabs
        res_specs = [pl.BlockSpec((tm, tn), lambda j, i: (jnp.where(j == 0, 0, i), jnp.maximum(j - 1, 0))),
                     pl.BlockSpec((ms, tn), lambda j, i: (0, jnp.maximum(j - 1, 0)))]
        side_args += list(residual)
    row_tile = lambda j, i: jnp.where(j == 0, 0, i)
    col_tile = lambda j, i: jnp.maximum(j - 1, 0)
    assert not (tiled_out and slabs)
    if tiled_out:
        shapes = [(nj, m, tn), (nj, ms, tn)]
        o_spec = pl.BlockSpec((None, tm, tn), lambda j, i: (col_tile(j, i), row_tile(j, i), 0))
        os_spec = pl.BlockSpec((None, ms, tn), lambda j, i: (col_tile(j, i), 0, 0))
    elif slabs:
        per_tile = tn // HEAD_DIM
        shapes = [(n // HEAD_DIM, m, HEAD_DIM), (n // HEAD_DIM, ms, HEAD_DIM)]
        o_spec = pl.BlockSpec((per_tile, tm, HEAD_DIM), lambda j, i: (col_tile(j, i), row_tile(j, i), 0))
        os_spec = pl.BlockSpec((per_tile, ms, HEAD_DIM), lambda j, i: (col_tile(j, i), 0, 0))
    else:
        shapes = [(m, n), (ms, n)]
        o_spec = pl.BlockSpec((tm, tn), lambda j, i: (row_tile(j, i), col_tile(j, i)))
        os_spec = pl.BlockSpec((ms, tn), lambda j, i: (0, col_tile(j, i)))
    return pl.pallas_call(
        functools.partial(_mm_kernel, act=act, act_tiles=act_tiles, ck=ck, nj=nj, ni=ni,
                          side_blocks=tuple(side_blocks), cast_blocks=cast_blocks, slabs=slabs,
                          residual=residual is not None),
        out_shape=(*[jax.ShapeDtypeStruct(s, out_dtype) for s in shapes],
                   *[jax.ShapeDtypeStruct(c.shape, c.dtype) for c in side_rolls], *cast_shapes),
        grid=(nj + 1, ni),
        in_specs=[
            pl.BlockSpec((tm, kdim), lambda j, i: (row_tile(j, i), 0)),
            pl.BlockSpec((None, ck, tn),
                         lambda j, i: (layer, jnp.where(j < nj, i, ni - 1), jnp.minimum(j, nj - 1))),
            pl.BlockSpec((1, tn), lambda j, i: (0, col_tile(j, i))),
            pl.BlockSpec((ms, kdim), lambda j, i: (0, 0)),
            *side_in_specs, *cast_in_specs, *res_specs,
        ],
        out_specs=(o_spec, os_spec, *side_out_specs, *cast_out_specs),
        scratch_shapes=[pltpu.VMEM((2, kdim, tn), _BF16)],
        compiler_params=_params("arbitrary", "arbitrary"),
        name="matmul",
    )(x, w, b, xs, *side_args)


def matmul_ktiled(x, xs, w, b, side_rolls=()):
    nkt, m, tk = x.shape
    ms = xs.shape[1]
    n = w.shape[1]
    tm = min(MM_KACC_TILE_M, m)
    tn = min(MM_TILE, n)
    n_sub = min(MM_KACC_SUBTILES, nkt)
    assert m % tm == 0 and n % tn == 0 and nkt % n_sub == 0 and w.shape[0] == nkt * tk
    nj, ni, nk = n // tn, m // tm, nkt // n_sub
    side_in_specs, side_out_specs, side_blocks, side_args = [], [], [], []
    for cache in side_rolls:
        tl = _roll_rows(cache, nj * ni * nk)
        cur, nxt, n_side = _side_roll_specs(cache, tl, lambda j, i, k: (j * ni + i) * nk + k)
        side_in_specs += [cur, nxt]
        side_out_specs.append(cur)
        side_blocks.append(n_side)
        side_args += [cache, cache]
    return pl.pallas_call(
        functools.partial(_mm_acc_kernel, n_sub=n_sub, ni=ni, nk=nk, side_blocks=tuple(side_blocks)),
        out_shape=(jax.ShapeDtypeStruct((m, n), _F32), jax.ShapeDtypeStruct((ms, n), _F32),
                   *[jax.ShapeDtypeStruct(c.shape, c.dtype) for c in side_rolls]),
        grid=(nj, ni, nk),
        in_specs=[
            pl.BlockSpec((n_sub, tm, tk), lambda j, i, k: (k, i, 0)),
            pl.BlockSpec((n_sub * tk, tn), lambda j, i, k: (k, j)),
            pl.BlockSpec((1, tn), lambda j, i, k: (0, j)),
            pl.BlockSpec((n_sub, ms, tk), lambda j, i, k: (k, 0, 0)),
            *side_in_specs,
        ],
        out_specs=(pl.BlockSpec((tm, tn), lambda j, i, k: (i, j)),
                   pl.BlockSpec((ms, tn), lambda j, i, k: (0, j)), *side_out_specs),
        compiler_params=_params("arbitrary", "arbitrary", "arbitrary"),
        name="matmul_ktiled",
    )(x, w, b, xs, *side_args)


def _ln_kernel(*refs, fused_residual):
    if fused_residual:
        z_ref, g_ref, b_ref, o_ref, obf_ref = refs
        z = z_ref[...]
    else:
        x_ref, y_ref, g_ref, b_ref, o_ref, obf_ref = refs
        z = ALPHA * x_ref[...] + y_ref[...]
    mu = jnp.mean(z, axis=-1, keepdims=True)
    zc = z - mu
    var = jnp.mean(zc * zc, axis=-1, keepdims=True)
    out = zc * lax.rsqrt(var + LN_EPS) * g_ref[...] + b_ref[...]
    o_ref[...] = out
    obf_ref[...] = out.astype(_BF16)


def deepnorm(x, y, g, b):
    m, d = y.shape
    tm = min(LN_ROWS, m)
    assert m % tm == 0
    row = pl.BlockSpec((tm, d), lambda i: (i, 0))
    vec = pl.BlockSpec((1, d), lambda i: (0, 0))
    rows_in = [y] if x is None else [x, y]
    return pl.pallas_call(
        functools.partial(_ln_kernel, fused_residual=x is None),
        out_shape=(jax.ShapeDtypeStruct((m, d), _F32), jax.ShapeDtypeStruct((m, d), _BF16)),
        grid=(m // tm,),
        in_specs=[row] * len(rows_in) + [vec, vec],
        out_specs=(row, row),
        compiler_params=_params("arbitrary"),
        name="deepnorm",
    )(*rows_in, g, b)


def _softplus(x):
    return jnp.maximum(x, 0.0) + jnp.log1p(jnp.exp(-jnp.abs(x)))


def _lru_coeffs(xc, wra, bra, wix, bix, sp):
    xcb = xc.astype(_BF16)
    r = jax.nn.sigmoid(jnp.dot(xcb, wra, preferred_element_type=_F32) + bra)
    i = jax.nn.sigmoid(jnp.dot(xcb, wix, preferred_element_type=_F32) + bix)
    log_a = -LRU_C * r * sp
    a = jnp.exp(log_a)
    t = jnp.tanh(log_a)
    bt = jnp.sqrt(-2.0 * t / (1.0 - t)) * (i * xc)
    return a, bt


def _shift_rows(x, s, fill):
    rows = lax.broadcasted_iota(jnp.int32, x.shape, 0)
    return jnp.where(rows < s, fill, pltpu.roll(x, s, 0))


def _rglru_kernel(xb_ref, gate_ref, cw_ref, cb_ref, wra_ref, bra_ref, wix_ref, bix_ref, lam_ref,
                  o_ref, hlast_ref, xpad_ref, *, seq, chunk):
    pad = V7X_SUBLANES
    cw = cw_ref[...]
    cbias = cb_ref[...]
    wra = wra_ref[...].astype(_BF16)
    wix = wix_ref[...].astype(_BF16)
    bra = bra_ref[...]
    bix = bix_ref[...]
    sp = _softplus(-lam_ref[...])

    xpad_ref[pl.ds(0, pad), :] = jnp.zeros((pad, xpad_ref.shape[1]), _F32)
    xpad_ref[pl.ds(pad, seq), :] = xb_ref[...]

    def body(c, h):
        t0 = pl.multiple_of(c * chunk, chunk)
        xw = xpad_ref[pl.ds(t0, chunk + pad), :]
        xc = cbias
        for k in range(CONV_W):
            off = pad - (CONV_W - 1) + k
            xc = xc + cw[k:k + 1, :] * xw[off:off + chunk, :]
        a, bt = _lru_coeffs(xc, wra, bra, wix, bix, sp)
        hs = []
        for q in range(chunk // SCAN_SUB):
            aq = a[q * SCAN_SUB:(q + 1) * SCAN_SUB]
            bq = bt[q * SCAN_SUB:(q + 1) * SCAN_SUB]
            s = 1
            while s < SCAN_SUB:
                if s < V7X_SUBLANES:
                    a_sh = _shift_rows(aq, s, 1.0)
                    b_sh = _shift_rows(bq, s, 0.0)
                    bq = aq * b_sh + bq
                    aq = aq * a_sh
                else:
                    bq = jnp.concatenate([bq[:s], aq[s:] * bq[:-s] + bq[s:]], axis=0)
                    aq = jnp.concatenate([aq[:s], aq[s:] * aq[:-s]], axis=0)
                s *= 2
            hq = aq * h + bq
            h = hq[SCAN_SUB - 1:SCAN_SUB, :]
            hs.append(hq)
        hs = jnp.concatenate(hs, axis=0)
        o_ref[pl.ds(t0, chunk), :] = (hs * gate_ref[pl.ds(t0, chunk), :]).astype(o_ref.dtype)
        return h

    h0 = jnp.zeros((1, xb_ref.shape[1]), _F32)
    hlast_ref[...] = lax.fori_loop(0, seq // chunk, body, h0)


def rglru_prompt(u, cw, cb, wra, bra, wix, bix, lam):
    bsz, seq, c2 = u.shape
    c = c2 // 2
    nblk, blk, _ = wra.shape
    assert c == nblk * blk and seq % SCAN_CHUNK == 0
    col = lambda b, n: (b, 0, n)
    vec = pl.BlockSpec((1, blk), lambda b, n: (0, n))
    wspec = pl.BlockSpec((None, blk, blk), lambda b, n: (n, 0, 0))
    return pl.pallas_call(
        functools.partial(_rglru_kernel, seq=seq, chunk=SCAN_CHUNK),
        out_shape=(jax.ShapeDtypeStruct((bsz, seq, c), _BF16), jax.ShapeDtypeStruct((bsz, 1, c), _F32)),
        grid=(bsz, nblk),
        in_specs=[
            pl.BlockSpec((None, seq, blk), lambda b, n: (b, 0, n + nblk)),
            pl.BlockSpec((None, seq, blk), col),
            pl.BlockSpec((CONV_W, blk), lambda b, n: (0, n)),
            vec, wspec, vec, wspec, vec, vec,
        ],
        out_specs=(pl.BlockSpec((None, seq, blk), col), pl.BlockSpec((None, 1, blk), col)),
        scratch_shapes=[pltpu.VMEM((seq + V7X_SUBLANES, blk), _F32)],
        compiler_params=_params("arbitrary", "arbitrary"),
        name="rglru_prompt",
    )(u, u, cw, cb, wra, bra, wix, bix, lam)


def _rglru_step_kernel(xb_ref, gate_ref, c0_ref, c1_ref, c2_ref, h_ref, cw_ref, cb_ref,
                       wra_ref, bra_ref, wix_ref, bix_ref, lam_ref, o_ref, hnew_ref):
    cw = cw_ref[...]
    xc = (cb_ref[...] + cw[0:1, :] * c0_ref[...] + cw[1:2, :] * c1_ref[...]
          + cw[2:3, :] * c2_ref[...] + cw[3:4, :] * xb_ref[...])
    a, bt = _lru_coeffs(xc, wra_ref[...].astype(_BF16), bra_ref[...], wix_ref[...].astype(_BF16),
                        bix_ref[...], _softplus(-lam_ref[...]))
    h = a * h_ref[...] + bt
    hnew_ref[...] = h
    o_ref[...] = (h * gate_ref[...]).astype(o_ref.dtype)


def rglru_step(u, conv_prev, h_prev, cw, cb, wra, bra, wix, bix, lam):
    rows, c2 = u.shape
    c = c2 // 2
    nblk, blk, _ = wra.shape
    col = pl.BlockSpec((rows, blk), lambda n: (0, n))
    vec = pl.BlockSpec((1, blk), lambda n: (0, n))
    wspec = pl.BlockSpec((None, blk, blk), lambda n: (n, 0, 0))
    return pl.pallas_call(
        _rglru_step_kernel,
        out_shape=(jax.ShapeDtypeStruct((rows, c), _BF16), jax.ShapeDtypeStruct((rows, c), _F32)),
        grid=(nblk,),
        in_specs=[pl.BlockSpec((rows, blk), lambda n: (0, n + nblk)), col, col, col, col, col,
                  pl.BlockSpec((CONV_W, blk), lambda n: (0, n)), vec, wspec, vec, wspec, vec, vec],
        out_specs=(col, col),
        compiler_params=_params("arbitrary"),
        name="rglru_step",
    )(u, u, *conv_prev, h_prev, cw, cb, wra, bra, wix, bix, lam)


def _attn_prompt_kernel(*refs, seq):
    qkv_refs = refs[:9]
    o_ref = refs[9]
    acc_ref, m_ref, l_ref, stage_in_ref, stage_out_ref = refs[10:15]
    kb = KEYS_PER_BLOCK
    scale = HEAD_DIM ** -0.5
    qi = lax.broadcasted_iota(jnp.int32, (kb, kb), 0)
    ki = lax.broadcasted_iota(jnp.int32, (kb, kb), 1)
    cur_mask = ki <= qi
    qi2 = lax.broadcasted_iota(jnp.int32, (kb, 2 * kb), 0)
    ki2 = lax.broadcasted_iota(jnp.int32, (kb, 2 * kb), 1)
    cur_part = jnp.logical_and(ki2 >= kb, ki2 - kb <= qi2)
    prev_part = jnp.logical_and(ki2 < kb, ki2 >= qi2)
    n_blocks = seq // kb
    assert n_blocks % ATTN_UNROLL == 0

    def strided(start, n, st):
        return pl.ds(start, n) if st == 1 else pl.ds(start, n, stride=st)

    def run_group(load, store, st, seg_len):
        per_res = seg_len // st
        nblk = per_res // kb
        assert per_res % kb == 0 and n_blocks % (st * nblk) == 0

        def block_rows(idx):
            seg = idx // (st * nblk)
            r = (idx // nblk) % st
            j = idx % nblk
            first = seg * seg_len + r
            start = first + j * (kb * st)
            prows = strided(jnp.maximum(start - kb * st, first), kb, st)
            return strided(start, kb, st), prows, j

        def scores(rows, prows, j):
            q = load(0, rows).astype(_BF16)
            k = load(1, rows).astype(_BF16)
            if nblk > 1:
                k = jnp.concatenate([load(1, prows).astype(_BF16), k], axis=0)
                mask = jnp.logical_or(cur_part, jnp.logical_and(prev_part, j > 0))
            else:
                mask = cur_mask
            s = lax.dot_general(q, k, (((1,), (1,)), ((), ())), preferred_element_type=_F32) * scale
            return jnp.where(mask, s, -jnp.inf)

        def values(rows, prows):
            v = load(2, rows).astype(_BF16)
            if nblk > 1:
                v = jnp.concatenate([load(2, prows).astype(_BF16), v], axis=0)
            return v

        def blocks(it, carry):
            where = [block_rows(it * ATTN_UNROLL + u) for u in range(ATTN_UNROLL)]
            ss = [scores(rows, prows, j) for rows, prows, j in where]
            ms = [jnp.max(s, axis=-1, keepdims=True) for s in ss]
            ps = [jnp.exp(s - m) for s, m in zip(ss, ms)]
            for (rows, prows, _), m, p in zip(where, ms, ps):
                store(0, rows, jnp.dot(p.astype(_BF16), values(rows, prows), preferred_element_type=_F32))
                store(1, rows, jnp.broadcast_to(m, (kb, HEAD_DIM)))
                store(2, rows, jnp.broadcast_to(jnp.sum(p, axis=-1, keepdims=True), (kb, HEAD_DIM)))
            return carry

        lax.fori_loop(0, n_blocks // ATTN_UNROLL, blocks, 0)

    stats = (acc_ref, m_ref, l_ref)
    for g, (_, dil) in enumerate(GROUPS):
        src = qkv_refs[3 * g:3 * g + 3]

        def store_tok(a, rows, val, g=g):
            stats[a][g, rows, :] = val

        if dil <= ATTN_MAX_STRIDE:
            run_group(lambda a, rows, src=src: src[a][rows, :], store_tok, dil, seq)
            continue
        inner = ATTN_MAX_STRIDE
        outer = dil // inner
        assert dil == inner * outer
        seg_len = seq // outer
        for a in range(3):
            for r1 in range(outer):
                for c in range(seg_len // kb):
                    stage_in_ref[a, pl.ds(r1 * seg_len + c * kb, kb), :] = (
                        src[a][strided(r1 + c * kb * outer, kb, outer), :])

        def store_seg(a, rows, val):
            stage_out_ref[a, rows, :] = val

        run_group(lambda a, rows: stage_in_ref[a, rows, :], store_seg, inner, seg_len)
        for a in range(3):
            for r1 in range(outer):
                for c in range(seg_len // kb):
                    stats[a][g, strided(r1 + c * kb * outer, kb, outer), :] = (
                        stage_out_ref[a, pl.ds(r1 * seg_len + c * kb, kb), :])

    def merge(c, carry):
        t0 = pl.multiple_of(c * kb, kb)
        ms = [m_ref[g, pl.ds(t0, kb), :] for g in range(len(GROUPS))]
        mx = functools.reduce(jnp.maximum, ms)
        num = jnp.zeros((kb, HEAD_DIM), _F32)
        den = jnp.zeros((kb, HEAD_DIM), _F32)
        for g in range(len(GROUPS)):
            w = jnp.exp(ms[g] - mx)
            num = num + w * acc_ref[g, pl.ds(t0, kb), :]
            den = den + w * l_ref[g, pl.ds(t0, kb), :]
        o_ref[pl.ds(t0, kb), :] = (num / den).astype(o_ref.dtype)
        return carry

    lax.fori_loop(0, seq // kb, merge, 0)


def attn_prompt(qkv):
    n_slabs, bsz, seq, _ = qkv.shape
    ng = len(GROUPS)
    n_heads = n_slabs // (3 * ng)

    def spec(section):
        return pl.BlockSpec((None, None, seq, HEAD_DIM), lambda b, h: (section * n_heads + h, b, 0, 0))

    return pl.pallas_call(
        functools.partial(_attn_prompt_kernel, seq=seq),
        out_shape=jax.ShapeDtypeStruct((bsz, seq, n_heads * HEAD_DIM), _BF16),
        grid=(bsz, n_heads),
        in_specs=[spec(s) for s in range(3 * ng)],
        out_specs=pl.BlockSpec((None, seq, HEAD_DIM), lambda b, h: (b, 0, h)),
        scratch_shapes=[pltpu.VMEM((ng, seq, HEAD_DIM), _F32)] * 3 + [pltpu.VMEM((3, seq, HEAD_DIM), _F32)] * 2,
        compiler_params=_params("arbitrary", "arbitrary"),
        name="attn_prompt",
    )(*([qkv] * (3 * ng)))


def _attn_sample_kernel(*refs):
    ng = len(GROUPS)
    q_refs = refs[0:ng]
    knew_refs = refs[ng:2 * ng]
    vnew_refs = refs[2 * ng:3 * ng]
    kc_refs = refs[3 * ng:4 * ng]
    vc_refs = refs[4 * ng:5 * ng]
    o_ref = refs[5 * ng]
    scale = HEAD_DIM ** -0.5
    parts = []
    for g in range(ng):
        q = q_refs[g][...]
        s_new = jnp.sum(q * knew_refs[g][...], axis=-1, keepdims=True) * scale
        s_old = jnp.sum(q[None] * kc_refs[g][...], axis=-1, keepdims=True) * scale
        m = jnp.maximum(jnp.max(s_old, axis=0), s_new)
        p_new = jnp.exp(s_new - m)
        p_old = jnp.exp(s_old - m[None])
        l = p_new + jnp.sum(p_old, axis=0)
        acc = p_new * vnew_refs[g][...] + jnp.sum(p_old * vc_refs[g][...], axis=0)
        parts.append((m, l, acc))
    mx = functools.reduce(jnp.maximum, [p[0] for p in parts])
    num = sum(jnp.exp(m - mx) * acc for m, _, acc in parts)
    den = sum(jnp.exp(m - mx) * l for m, l, _ in parts)
    o_ref[...] = (num / den).astype(o_ref.dtype)


def attn_sample(qkv, caches, n_heads):
    bsz = qkv.shape[0]
    ng = len(GROUPS)
    kb = KEYS_PER_BLOCK

    def new_spec(section):
        return pl.BlockSpec((None, None, n_heads, HEAD_DIM), lambda b: (b, section, 0, 0))

    def cache_spec(kv):
        return pl.BlockSpec((None, kb, None, None, n_heads, HEAD_DIM), lambda b: (b, 0, 0, kv, 0, 0))

    strided = []
    for (win, dil), cache in zip(GROUPS, caches):
        assert cache.shape[1] == win == kb * dil
        strided.append(cache.reshape(bsz, kb, dil, 2, n_heads, HEAD_DIM))
    in_specs = ([new_spec(3 * g) for g in range(ng)] + [new_spec(3 * g + 1) for g in range(ng)]
                + [new_spec(3 * g + 2) for g in range(ng)] + [cache_spec(0)] * ng + [cache_spec(1)] * ng)
    return pl.pallas_call(
        _attn_sample_kernel,
        out_shape=jax.ShapeDtypeStruct((bsz, n_heads, HEAD_DIM), _BF16),
        grid=(bsz,),
        in_specs=in_specs,
        out_specs=pl.BlockSpec((None, n_heads, HEAD_DIM), lambda b: (b, 0, 0)),
        compiler_params=_params("arbitrary"),
        name="attn_sample",
    )(*([qkv] * (3 * ng)), *strided, *strided)


def _set_last_kernel(rolled_ref, new_ref, o_ref):
    del rolled_ref
    o_ref[...] = new_ref[...]


def cache_set_last(rolled, new_row):
    bsz, length = rolled.shape[:2]
    tail = rolled.shape[2:]
    zeros = (0,) * len(tail)
    return pl.pallas_call(
        _set_last_kernel,
        out_shape=jax.ShapeDtypeStruct(rolled.shape, rolled.dtype),
        grid=(bsz,),
        in_specs=[pl.BlockSpec(memory_space=pl.ANY),
                  pl.BlockSpec((None, 1) + tail, lambda b: (b, 0) + zeros)],
        out_specs=pl.BlockSpec((None, 1) + tail, lambda b: (b, length - 1) + zeros),
        input_output_aliases={0: 0},
        compiler_params=_params("arbitrary"),
        name="cache_set_last",
    )(rolled, new_row)


def _pad_rows(a, rows):
    return jnp.pad(a, ((0, rows - a.shape[0]),) + ((0, 0),) * (a.ndim - 1))


def kernel(x_prompt, x_sample, state_conv, state_h, cache_kv_w128, cache_kv_w512, cache_kv_w2048,
           lru_w_in, lru_b_in, lru_conv_w, lru_conv_b, lru_w_ra, lru_b_ra, lru_w_ix, lru_b_ix,
           lru_lambda, lru_w_out, lru_b_out, attn_w_qkv, attn_w_o,
           mlp_w1, mlp_b1, mlp_w2, mlp_b2, ln_g, ln_b):
    bp, seq, d = x_prompt.shape
    bs, dec_seq, _ = x_sample.shape
    assert dec_seq == 1
    n_heads = d // HEAD_DIM
    ng = len(GROUPS)
    mp = bp * seq
    ms = -(-bs // V7X_BF16_SUBLANES) * V7X_BF16_SUBLANES
    caches = (cache_kv_w128[0], cache_kv_w512[0], cache_kv_w2048[0])
    row = lambda v: v[:, None, :]
    b_in, conv_b, b_ra, b_ix, lam, b_out = (row(v) for v in (lru_b_in, lru_conv_b, lru_b_ra, lru_b_ix,
                                                               lru_lambda, lru_b_out))
    b1, b2 = row(mlp_b1), row(mlp_b2)
    g_ln, b_ln = ln_g[:, :, None, :], ln_b[:, :, None, :]

    xp = x_prompt.reshape(mp, d)
    xs = _pad_rows(x_sample.reshape(bs, d), ms)

    def norm(xf, y, i, k):
        return deepnorm(xf, y, g_ln[i][k], b_ln[i][k])

    def mlp(xfp, xbp, xfs, xbs, i, side_rolls=()):
        hp, hs, w2_bf = matmul(xbp, xbs, mlp_w1, i, b1[i], act="relu2", out_dtype=_BF16,
                               tiled_out=True, side_cast=(mlp_w2, i))
        yp, ys, *rolled = matmul_ktiled(hp, hs, w2_bf, b2[i], side_rolls=side_rolls)
        return norm(xfp, yp, i, 1), norm(xfs, ys, i, 1), rolled

    d_rnn = lru_conv_w.shape[2]
    up, us = matmul(xp.astype(_BF16), xs.astype(_BF16), lru_w_in, 0, b_in[0],
                    act="gelu", act_cols=d_rnn)
    lru_args = (lru_conv_w[0], conv_b[0], lru_w_ra[0], b_ra[0], lru_w_ix[0], b_ix[0], lam[0])
    up3 = up.reshape(bp, seq, 2 * d_rnn)
    hgp, h_p = rglru_prompt(up3, *lru_args)
    conv_p = up3[:, seq - (CONV_W - 1):, d_rnn:]
    conv_prev = [_pad_rows(state_conv[0][:, k], ms) for k in range(CONV_W - 1)]
    hgs, h_s = rglru_step(us, conv_prev, _pad_rows(state_h[0], ms), *lru_args)
    conv_s = jnp.concatenate([state_conv[0][:, 1:], us[:bs, None, d_rnn:]], axis=1)
    zp, zs = matmul(hgp.reshape(mp, d_rnn), hgs, lru_w_out, 0, b_out[0], residual=(xp, xs))
    (xfp, xbp), (xfs, xbs) = norm(None, zp, 0, 0), norm(None, zs, 0, 0)
    (xfp, xbp), (xfs, xbs), rolled_small = mlp(xfp, xbp, xfs, xbs, 0, side_rolls=caches[:-1])

    zero_bias = jnp.zeros((1, attn_w_qkv.shape[2]), _F32)
    qkv_p, qkv_s, rolled_big = matmul(xbp, xbs, attn_w_qkv, 0, zero_bias, slabs=True, side_rolls=caches[-1:])
    qkv_p = qkv_p.reshape(3 * ng * n_heads, bp, seq, HEAD_DIM)
    op = attn_prompt(qkv_p).reshape(mp, d)
    kv_p = []
    for g, (win, _) in enumerate(GROUPS):
        keep = min(win, seq)
        kv = qkv_p[(3 * g + 1) * n_heads:(3 * g + 3) * n_heads, :, seq - keep:]
        kv_p.append(jnp.transpose(kv, (1, 2, 0, 3)).reshape(1, bp, keep, 2, n_heads, HEAD_DIM))
    qkv_s = jnp.transpose(qkv_s[:, :bs], (1, 0, 2)).reshape(bs, 3 * ng, n_heads, HEAD_DIM)
    os_ = _pad_rows(attn_sample(qkv_s, caches, n_heads).reshape(bs, d), ms)
    zp, zs = matmul(op, os_, attn_w_o, 0, jnp.zeros((1, d), _F32), residual=(xfp, xfs))
    (xfp, xbp), (xfs, xbs) = norm(None, zp, 1, 0), norm(None, zs, 1, 0)
    (xfp, _), (xfs, _), _ = mlp(xfp, xbp, xfs, xbs, 1)

    rolled = (*rolled_small, rolled_big)
    kv_s = []
    for g in range(ng):
        new_row = qkv_s[:, 3 * g + 1:3 * g + 3].reshape(bs, 1, 2, n_heads, HEAD_DIM)
        kv_s.append(cache_set_last(rolled[g], new_row)[None])

    y_p = xfp.reshape(bp, seq, d)
    y_s = xfs[:bs].reshape(bs, 1, d)
    return (y_p, y_s, conv_p[None], h_p.reshape(1, bp, d_rnn), kv_p[0], kv_p[1], kv_p[2],
            conv_s[None], h_s[:bs][None], kv_s[0], kv_s[1], kv_s[2])
```
